```python
import math
import jax, jax.numpy as jnp
from jax import lax
import numpy as np

D_MODEL = 1024
BATCH = 8
SEQ = 2048
DEPTH = 2

GRID_W = 64
HEAD_DIM = 64
Q_BLOCK = 128
EPS = 1e-6
NEG_INF = -1e30
DIFF_HEADS = 4
DIFF_V_DIM = 2 * HEAD_DIM
DIFF_QK_W = DIFF_HEADS * 2 * HEAD_DIM
DIFF_V_W = DIFF_HEADS * DIFF_V_DIM
NA_HEADS = 8
NA_WIN_H = 8
NA_WIN_W = 16
NA_QCOL_BLOCK = 16
NA_KCOL_SPAN = NA_QCOL_BLOCK + NA_WIN_W
NA_W = NA_HEADS * HEAD_DIM
IN_PROJ_W = 2 * DIFF_QK_W + DIFF_V_W + 3 * NA_W
MIX_W = DIFF_V_W + NA_W
MLA_HEADS = 16
MLA_NOPE = 64
MLA_ROPE = 32
MLA_V = 64
MLA_Q_RANK = 512
MLA_KV_RANK = 256
ROPE_THETA = 10000.0
D_FF = 2816
CONV_W = 3
N_EVEN = (DEPTH + 1) // 2
N_ODD = DEPTH // 2

kernel_name = "hybrid_diffattn_natten_mla_convffn_encoder"


def rms_norm(x, g):
    xf = x.astype(jnp.float32)
    y = xf * lax.rsqrt(jnp.mean(xf * xf, axis=-1, keepdims=True) + EPS)
    return (y * g.astype(jnp.float32)).astype(x.dtype)


def alibi_slopes(n):
    return jnp.asarray(2.0 ** (-8.0 * np.arange(1, n + 1) / n), dtype=jnp.float32)


def rope_tables(s, dim):
    inv = ROPE_THETA ** (-jnp.arange(0, dim, 2, dtype=jnp.float32) / dim)
    ang = jnp.arange(s, dtype=jnp.float32)[:, None] * inv[None, :]
    return jnp.cos(ang), jnp.sin(ang)


def apply_rope(x, cos, sin):
    xf = x.astype(jnp.float32)
    half = x.shape[-1] // 2
    x1, x2 = xf[..., :half], xf[..., half:]
    return jnp.concatenate([x1 * cos - x2 * sin, x2 * cos + x1 * sin], axis=-1).astype(x.dtype)


def diff_attention(q, k, v, lam, slopes):
    b, s, h, _, dh = q.shape
    nblk = s // Q_BLOCK
    scale = dh ** -0.5
    pos = jnp.arange(s)
    qb = jnp.moveaxis(q.reshape(b, nblk, Q_BLOCK, h, 2, dh), 1, 0)

    def block(args):
        qi, i = args
        tq = i * Q_BLOCK + jnp.arange(Q_BLOCK)
        dist = jnp.abs(tq[:, None] - pos[None, :]).astype(jnp.float32)
        bias = -slopes[:, None, None] * dist[None]
        sc = jnp.einsum('bqhnd,bshnd->bhnqs', qi, k).astype(jnp.float32) * scale + bias[None, :, None]
        p = jax.nn.softmax(sc, axis=-1)
        a = (p[:, :, 0] - lam * p[:, :, 1]).astype(v.dtype)
        return jnp.einsum('bhqs,bshe->bqhe', a, v)

    o = lax.map(block, (qb, jnp.arange(nblk)))
    return jnp.moveaxis(o, 0, 1).reshape(b, s, h, v.shape[-1])


def neighbourhood_attention(q, k, v, rpb):
    b, s, h, dh = q.shape
    rows_n = s // GRID_W
    kh = min(NA_WIN_H, rows_n)
    nb = GRID_W // NA_QCOL_BLOCK
    scale = dh ** -0.5
    rows = np.arange(rows_n)
    row_start = np.clip(rows - kh // 2, 0, rows_n - kh)
    row_idx = row_start[:, None] + np.arange(kh)[None, :]
    cblk = np.clip(np.arange(nb) * NA_QCOL_BLOCK - NA_WIN_W // 2, 0, GRID_W - NA_KCOL_SPAN)
    col_idx = cblk[:, None] + np.arange(NA_KCOL_SPAN)[None, :]
    qcol = np.arange(nb)[:, None] * NA_QCOL_BLOCK + np.arange(NA_QCOL_BLOCK)[None, :]
    wstart = np.clip(qcol - NA_WIN_W // 2, 0, GRID_W - NA_WIN_W)
    kcol = col_idx[:, None, :]
    valid = (kcol >= wstart[..., None]) & (kcol < wstart[..., None] + NA_WIN_W)
    dr = row_idx - rows[:, None] + NA_WIN_H - 1
    dc = np.clip(kcol - qcol[..., None] + NA_WIN_W - 1, 0, 2 * NA_WIN_W - 2)
    bias = rpb[:, dr[:, :, None, None, None], dc[None, None]]
    bias = bias.transpose(1, 0, 3, 4, 2, 5).astype(jnp.float32)
    kg = k.reshape(b, rows_n, GRID_W, h, dh)
    vg = v.reshape(b, rows_n, GRID_W, h, dh)
    qrows = jnp.moveaxis(q.reshape(b, rows_n, nb, NA_QCOL_BLOCK, h, dh), 1, 0)

    def row(args):
        qr, rs, br = args
        kr = lax.dynamic_slice_in_dim(kg, rs, kh, axis=1)[:, :, col_idx]
        vr = lax.dynamic_slice_in_dim(vg, rs, kh, axis=1)[:, :, col_idx]
        sc = jnp.einsum('bcqhd,bkcjhd->bhcqkj', qr, kr).astype(jnp.float32) * scale + br[None]
        sc = jnp.where(valid[:, :, None, :], sc, NEG_INF)
        p = jax.nn.softmax(sc.reshape(sc.shape[:4] + (kh * NA_KCOL_SPAN,)), axis=-1)
        p = p.reshape(sc.shape).astype(v.dtype)
        return jnp.einsum('bhcqkj,bkcjhd->bcqhd', p, vr)

    o = lax.map(row, (qrows, jnp.asarray(row_start, dtype=jnp.int32), bias))
    return jnp.moveaxis(o, 0, 1).reshape(b, s, h * dh)


def diff_na_mixer(h, w_in, lq1, lk1, lq2, lk2, subln_g, rpb, w_out, layer):
    b, s, _ = h.shape
    proj = h @ w_in
    cuts = np.cumsum([DIFF_QK_W, DIFF_QK_W, DIFF_V_W, NA_W, NA_W]).tolist()
    qd, kd, vd, qn, kn, vn = jnp.split(proj, cuts, axis=-1)
    lam_init = 0.8 - 0.6 * math.exp(-0.3 * layer)
    f32 = lambda t: t.astype(jnp.float32)
    lam = jnp.exp(jnp.sum(f32(lq1) * f32(lk1))) - jnp.exp(jnp.sum(f32(lq2) * f32(lk2))) + lam_init
    a = diff_attention(qd.reshape(b, s, DIFF_HEADS, 2, HEAD_DIM),
                       kd.reshape(b, s, DIFF_HEADS, 2, HEAD_DIM),
                       vd.reshape(b, s, DIFF_HEADS, DIFF_V_DIM), lam, alibi_slopes(DIFF_HEADS))
    a = (rms_norm(a, subln_g) * (1.0 - lam_init)).reshape(b, s, DIFF_V_W)
    nb_out = neighbourhood_attention(qn.reshape(b, s, NA_HEADS, HEAD_DIM),
                                     kn.reshape(b, s, NA_HEADS, HEAD_DIM),
                                     vn.reshape(b, s, NA_HEADS, HEAD_DIM), rpb)
    return jnp.concatenate([a, nb_out], axis=-1) @ w_out


def mla_attention(h, w_dq, q_norm_g, w_uq, w_dkv, kv_norm_g, w_ukv, w_o):
    b, s, _ = h.shape
    nblk = s // Q_BLOCK
    scale = (MLA_NOPE + MLA_ROPE) ** -0.5
    cq = rms_norm(h @ w_dq, q_norm_g)
    q = (cq @ w_uq).reshape(b, s, MLA_HEADS, MLA_NOPE + MLA_ROPE)
    q_nope, q_rope = q[..., :MLA_NOPE], q[..., MLA_NOPE:]
    kv_a = h @ w_dkv
    ckv = rms_norm(kv_a[..., :MLA_KV_RANK], kv_norm_g)
    k_rope = kv_a[..., MLA_KV_RANK:]
    kv = (ckv @ w_ukv).reshape(b, s, MLA_HEADS, MLA_NOPE + MLA_V)
    k_nope, v = kv[..., :MLA_NOPE], kv[..., MLA_NOPE:]
    cos, sin = rope_tables(s, MLA_ROPE)
    q_rope = apply_rope(q_rope, cos[:, None, :], sin[:, None, :])
    k_rope = apply_rope(k_rope, cos, sin)
    qnb = jnp.moveaxis(q_nope.reshape(b, nblk, Q_BLOCK, MLA_HEADS, MLA_NOPE), 1, 0)
    qrb = jnp.moveaxis(q_rope.reshape(b, nblk, Q_BLOCK, MLA_HEADS, MLA_ROPE), 1, 0)

    def block(args):
        qn, qr = args
        sc = (jnp.einsum('bqhd,bshd->bhqs', qn, k_nope)
              + jnp.einsum('bqhr,bsr->bhqs', qr, k_rope)).astype(jnp.float32) * scale
        p = jax.nn.softmax(sc, axis=-1).astype(v.dtype)
        return jnp.einsum('bhqs,bshd->bqhd', p, v)

    o = lax.map(block, (qnb, qrb))
    return jnp.moveaxis(o, 0, 1).reshape(b, s, MLA_HEADS * MLA_V) @ w_o


def conv_ffn(h, w_gate, w_val, conv_w, conv_b, w_down):
    a = h @ w_gate
    a = lax.conv_general_dilated(a, conv_w[:, None, :].astype(a.dtype), window_strides=(1,),
                                 padding=((CONV_W // 2, CONV_W // 2),),
                                 dimension_numbers=('NWC', 'WIO', 'NWC'),
                                 feature_group_count=D_FF) + conv_b
    return (jax.nn.gelu(a, approximate=False) * (h @ w_val)) @ w_down


def setup_inputs(seed: int = 0) -> dict:
    key = jax.random.key(seed)
    ks = iter(jax.random.split(key, 32))
    f = jnp.float32

    def w(shape, fan_in):
        return jax.random.normal(next(ks), shape, f) * fan_in ** -0.5

    def gain(shape):
        return 1.0 + 0.05 * jax.random.normal(next(ks), shape, f)

    def small(shape, std):
        return std * jax.random.normal(next(ks), shape, f)

    return {
        "x": jax.random.normal(next(ks), (BATCH, SEQ, D_MODEL), f),
        "mix_norm_e": gain((N_EVEN, D_MODEL)),
        "w_in_e": w((N_EVEN, D_MODEL, IN_PROJ_W), D_MODEL),
        "diff_lq1": small((N_EVEN, HEAD_DIM), 0.1),
        "diff_lk1": small((N_EVEN, HEAD_DIM), 0.1),
        "diff_lq2": small((N_EVEN, HEAD_DIM), 0.1),
        "diff_lk2": small((N_EVEN, HEAD_DIM), 0.1),
        "diff_subln_g": gain((N_EVEN, DIFF_V_DIM)),
        "na_rpb": small((N_EVEN, NA_HEADS, 2 * NA_WIN_H - 1, 2 * NA_WIN_W - 1), 0.1),
        "w_out_e": w((N_EVEN, MIX_W, D_MODEL), MIX_W),
        "mix_norm_o": gain((N_ODD, D_MODEL)),
        "w_dq": w((N_ODD, D_MODEL, MLA_Q_RANK), D_MODEL),
        "q_norm_g": gain((N_ODD, MLA_Q_RANK)),
        "w_uq": w((N_ODD, MLA_Q_RANK, MLA_HEADS * (MLA_NOPE + MLA_ROPE)), MLA_Q_RANK),
        "w_dkv": w((N_ODD, D_MODEL, MLA_KV_RANK + MLA_ROPE), D_MODEL),
        "kv_norm_g": gain((N_ODD, MLA_KV_RANK)),
        "w_ukv": w((N_ODD, MLA_KV_RANK, MLA_HEADS * (MLA_NOPE + MLA_V)), MLA_KV_RANK),
        "w_o_mla": w((N_ODD, MLA_HEADS * MLA_V, D_MODEL), MLA_HEADS * MLA_V),
        "ffn_norm_g": gain((DEPTH, D_MODEL)),
        "w_ffn_gate": w((DEPTH, D_MODEL, D_FF), D_MODEL),
        "w_ffn_val": w((DEPTH, D_MODEL, D_FF), D_MODEL),
        "ffn_conv_w": w((DEPTH, CONV_W, D_FF), CONV_W),
        "ffn_conv_b": small((DEPTH, D_FF), 0.02),
        "w_ffn_down": w((DEPTH, D_FF, D_MODEL), D_FF),
        "final_norm_g": gain((D_MODEL,)),
    }


def reference(x, mix_norm_e, w_in_e, diff_lq1, diff_lk1, diff_lq2, diff_lk2, diff_subln_g, na_rpb,
              w_out_e, mix_norm_o, w_dq, q_norm_g, w_uq, w_dkv, kv_norm_g, w_ukv, w_o_mla,
              ffn_norm_g, w_ffn_gate, w_ffn_val, ffn_conv_w, ffn_conv_b, w_ffn_down, final_norm_g):
    for i in range(DEPTH):
        j = i // 2
        if i % 2 == 0:
            h = rms_norm(x, mix_norm_e[j])
            x = x + diff_na_mixer(h, w_in_e[j], diff_lq1[j], diff_lk1[j], diff_lq2[j], diff_lk2[j],
                                  diff_subln_g[j], na_rpb[j], w_out_e[j], i)
        else:
            h = rms_norm(x, mix_norm_o[j])
            x = x + mla_attention(h, w_dq[j], q_norm_g[j], w_uq[j], w_dkv[j], kv_norm_g[j],
                                  w_ukv[j], w_o_mla[j])
        h = rms_norm(x, ffn_norm_g[i])
        x = x + conv_ffn(h, w_ffn_gate[i], w_ffn_val[i], ffn_conv_w[i], ffn_conv_b[i], w_ffn_down[i])
    return rms_norm(x, final_norm_g)
```

```python
import functools
import math

import numpy as np
import jax
import jax.numpy as jnp
from jax import lax
from jax.experimental import pallas as pl
from jax.experimental.pallas import tpu as pltpu

F32 = jnp.float32
BF16 = jnp.bfloat16

EPS = 1e-6
NEG_INF = -1e30
GRID_W = 64
HEAD_DIM = 64
DIFF_HEADS = 4
NA_HEADS = 8
NA_WIN_H = 8
NA_WIN_W = 16
MLA_HEADS = 16
MLA_NOPE = 64
MLA_ROPE = 32
MLA_V = 64
MLA_KV_RANK = 256
ROPE_THETA = 10000.0

LANES = 128
VMEM_LIMIT = 56 * 1024 * 1024
NT_DIMS = (((1,), (1,)), ((), ()))


def _cparams(*sem):
    return pltpu.CompilerParams(dimension_semantics=sem, vmem_limit_bytes=VMEM_LIMIT)


def _rms(xf, g):
    ms = jnp.mean(xf * xf, axis=-1, keepdims=True)
    return xf * lax.rsqrt(ms + EPS) * g


def _dot(a, b):
    return jnp.dot(a, b, preferred_element_type=F32)


def _softmax_parts(sc):
    m = jnp.max(sc, axis=-1, keepdims=True)
    e = jnp.exp(sc - m)
    return e, jnp.sum(e, axis=-1, keepdims=True)


def _low_lanes():
    return lax.broadcasted_iota(jnp.int32, (1, LANES), 1) < HEAD_DIM


def _norm_proj_body(x_ref, g_ref, w_ref, o_ref, *, n_chunk):
    hn = _rms(x_ref[...], g_ref[...]).astype(BF16)
    for j in range(0, o_ref.shape[1], n_chunk):
        o_ref[:, j:j + n_chunk] = _dot(hn, w_ref[:, j:j + n_chunk]).astype(o_ref.dtype)


def _norm_proj(x2, g, w, *, tm=512, n_chunk=768):
    t, d = x2.shape
    n = w.shape[1]
    return pl.pallas_call(
        functools.partial(_norm_proj_body, n_chunk=n_chunk),
        grid=(t // tm,),
        in_specs=[pl.BlockSpec((tm, d), lambda i: (i, 0)),
                  pl.BlockSpec((1, d), lambda i: (0, 0)),
                  pl.BlockSpec((d, n), lambda i: (0, 0))],
        out_specs=pl.BlockSpec((tm, n), lambda i: (i, 0)),
        out_shape=jax.ShapeDtypeStruct((t, n), BF16),
        compiler_params=_cparams("parallel"),
        name="norm_proj",
    )(x2, g.reshape(1, d), w)


def _diff_attn_body(slopes_ref, lvec_ref, q_ref, k_ref, v_ref, g_ref, o_ref, *, tq, lam_init):
    h = pl.program_id(1)
    qi = pl.program_id(2)
    q = q_ref[0]
    k = k_ref[0]
    v = v_ref[0]
    s_len = k.shape[0]
    lo = _low_lanes()
    zero = jnp.zeros_like(q)
    s1 = lax.dot_general(jnp.where(lo, q, zero), k, NT_DIMS, preferred_element_type=F32)
    s2 = lax.dot_general(jnp.where(lo, zero, q), k, NT_DIMS, preferred_element_type=F32)
    row = qi * tq + lax.broadcasted_iota(jnp.int32, (tq, s_len), 0)
    col = lax.broadcasted_iota(jnp.int32, (tq, s_len), 1)
    bias = (-slopes_ref[h]) * jnp.abs(row - col).astype(F32)
    scale = HEAD_DIM ** -0.5
    e1, l1 = _softmax_parts(s1 * scale + bias)
    e2, l2 = _softmax_parts(s2 * scale + bias)
    lv = lvec_ref[...]
    lam = (jnp.exp(jnp.sum(lv[0:1] * lv[1:2], axis=-1, keepdims=True))
           - jnp.exp(jnp.sum(lv[2:3] * lv[3:4], axis=-1, keepdims=True)) + lam_init)
    a = e1 * (1.0 / l1) - e2 * (lam / l2)
    o = _dot(a.astype(BF16), v)
    o_ref[0] = (_rms(o, g_ref[...]) * (1.0 - lam_init)).astype(o_ref.dtype)


def _diff_attention(proj, lvec, subln_g, *, lam_init, tq=256):
    b, s, _ = proj.shape
    slopes = jnp.asarray(2.0 ** (-8.0 * np.arange(1, DIFF_HEADS + 1) / DIFF_HEADS), dtype=F32)
    return pl.pallas_call(
        functools.partial(_diff_attn_body, tq=tq, lam_init=lam_init),
        grid=(b, DIFF_HEADS, s // tq),
        in_specs=[pl.BlockSpec(memory_space=pltpu.SMEM),
                  pl.BlockSpec((4, HEAD_DIM), lambda bi, h, qi: (0, 0)),
                  pl.BlockSpec((1, tq, LANES), lambda bi, h, qi: (bi, qi, h)),
                  pl.BlockSpec((1, s, LANES), lambda bi, h, qi: (bi, 0, DIFF_HEADS + h)),
                  pl.BlockSpec((1, s, LANES), lambda bi, h, qi: (bi, 0, 2 * DIFF_HEADS + h)),
                  pl.BlockSpec((1, LANES), lambda bi, h, qi: (0, 0))],
        out_specs=pl.BlockSpec((1, tq, LANES), lambda bi, h, qi: (bi, qi, h)),
        out_shape=jax.ShapeDtypeStruct((b, s, DIFF_HEADS * LANES), BF16),
        compiler_params=_cparams("parallel", "parallel", "parallel"),
        name="diff_attn",
    )(slopes, lvec, proj, proj, proj, subln_g.reshape(1, LANES))


def _na_bias_table(rpb, kh):
    d = np.arange(kh)[:, None]
    j = np.arange(kh)[None, :]
    dr = j - d + NA_WIN_H - 1
    qc = np.arange(GRID_W)[:, None]
    kc = np.arange(GRID_W)[None, :]
    wstart = np.clip(qc - NA_WIN_W // 2, 0, GRID_W - NA_WIN_W)
    valid = (kc >= wstart) & (kc < wstart + NA_WIN_W)
    dc = np.clip(kc - qc + NA_WIN_W - 1, 0, 2 * NA_WIN_W - 2)
    tab = rpb[:, dr[:, :, None, None], dc[None, None]]
    tab = jnp.where(valid[None, None, None], tab.astype(F32), NEG_INF)
    tab = tab.transpose(0, 1, 3, 2, 4)
    return tab.reshape(rpb.shape[0], kh, GRID_W, kh * GRID_W)


def _na_body(q_ref, k_ref, v_ref, bias_ref, o_ref, *, rows_n, kh):
    lo = _low_lanes()
    scale = HEAD_DIM ** -0.5

    def row(r, carry):
        rs = jnp.clip(r - kh // 2, 0, rows_n - kh)
        d = r - rs
        q = q_ref[0, pl.ds(pl.multiple_of(r * GRID_W, GRID_W), GRID_W), :]
        kwin = k_ref[0, pl.ds(pl.multiple_of(rs * GRID_W, GRID_W), kh * GRID_W), :]
        vwin = v_ref[0, pl.ds(pl.multiple_of(rs * GRID_W, GRID_W), kh * GRID_W), :]
        zero = jnp.zeros_like(q)
        outs = []
        for hh in range(2):
            qm = jnp.where(lo, q, zero) if hh == 0 else jnp.where(lo, zero, q)
            sc = lax.dot_general(qm, kwin, NT_DIMS, preferred_element_type=F32) * scale + bias_ref[hh, d]
            e, l = _softmax_parts(sc)
            outs.append(_dot(e.astype(BF16), vwin) * (1.0 / l))
        o_ref[0, pl.ds(pl.multiple_of(r * GRID_W, GRID_W), GRID_W), :] = (
            jnp.where(lo, outs[0], outs[1]).astype(o_ref.dtype))
        return carry

    lax.fori_loop(0, rows_n, row, 0)


def _na_attention(proj, rpb, *, col0):
    b, s, _ = proj.shape
    rows_n = s // GRID_W
    kh = min(NA_WIN_H, rows_n)
    pairs = NA_HEADS // 2
    bias = _na_bias_table(rpb, kh)
    return pl.pallas_call(
        functools.partial(_na_body, rows_n=rows_n, kh=kh),
        grid=(b, pairs),
        in_specs=[pl.BlockSpec((1, s, LANES), lambda bi, p: (bi, 0, col0 + p)),
                  pl.BlockSpec((1, s, LANES), lambda bi, p: (bi, 0, col0 + pairs + p)),
                  pl.BlockSpec((1, s, LANES), lambda bi, p: (bi, 0, col0 + 2 * pairs + p)),
                  pl.BlockSpec((2, kh, GRID_W, kh * GRID_W), lambda bi, p: (p, 0, 0, 0))],
        out_specs=pl.BlockSpec((1, s, LANES), lambda bi, p: (bi, 0, p)),
        out_shape=jax.ShapeDtypeStruct((b, s, pairs * LANES), BF16),
        compiler_params=_cparams("parallel", "parallel"),
        name="na_attn",
    )(proj, proj, proj, bias)


def _proj_res_body(*refs, n_in):
    x_ref, o_ref = refs[2 * n_in], refs[2 * n_in + 1]
    acc = x_ref[...]
    for a_ref, w_ref in zip(refs[:n_in], refs[n_in:2 * n_in]):
        acc = acc + _dot(a_ref[...], w_ref[...])
    o_ref[...] = acc


def _proj_residual(acts, ws, x2, *, tm=512):
    t, d = x2.shape
    n_in = len(acts)
    in_specs = ([pl.BlockSpec((tm, a.shape[1]), lambda i: (i, 0)) for a in acts]
                + [pl.BlockSpec(w.shape, lambda i: (0, 0)) for w in ws]
                + [pl.BlockSpec((tm, d), lambda i: (i, 0))])
    return pl.pallas_call(
        functools.partial(_proj_res_body, n_in=n_in),
        grid=(t // tm,),
        in_specs=in_specs,
        out_specs=pl.BlockSpec((tm, d), lambda i: (i, 0)),
        out_shape=jax.ShapeDtypeStruct((t, d), F32),
        compiler_params=_cparams("parallel"),
        name="proj_residual",
    )(*acts, *ws, x2)


def _ffn_body(x_ref, g_ref, wg_ref, wv_ref, cw_ref, cb_ref, wd_ref, fg_ref, o_ref, hn_ref, *, final_norm):
    c = pl.program_id(1)

    @pl.when(c == 0)
    def _():
        x = x_ref[0]
        hn_ref[...] = _rms(x, g_ref[...]).astype(BF16)
        o_ref[0] = x

    hn = hn_ref[...]
    a = _dot(hn, wg_ref[...])
    s_len = a.shape[0]
    t = lax.broadcasted_iota(jnp.int32, (s_len, 1), 0)
    a_prev = jnp.where(t == 0, 0.0, pltpu.roll(a, 1, 0))
    a_next = jnp.where(t == s_len - 1, 0.0, pltpu.roll(a, s_len - 1, 0))
    cw = cw_ref[...]
    a = cw[0:1] * a_prev + cw[1:2] * a + cw[2:3] * a_next + cb_ref[...]
    gate = 0.5 * a * (1.0 + lax.erf(a * np.float32(math.sqrt(0.5))))
    val = _dot(hn, wv_ref[...])
    o_ref[0] += _dot((gate * val).astype(BF16), wd_ref[...])

    if final_norm:
        @pl.when(c == pl.num_programs(1) - 1)
        def _():
            o_ref[0] = _rms(o_ref[0], fg_ref[...])


def _conv_ffn(x, g, w_gate, w_val, conv_w, conv_b, w_down, final_g, *, final_norm, fc=256):
    b, s, d = x.shape
    dff = w_gate.shape[1]
    return pl.pallas_call(
        functools.partial(_ffn_body, final_norm=final_norm),
        grid=(b, dff // fc),
        in_specs=[pl.BlockSpec((1, s, d), lambda bi, c: (bi, 0, 0)),
                  pl.BlockSpec((1, d), lambda bi, c: (0, 0)),
                  pl.BlockSpec((d, fc), lambda bi, c: (0, c)),
                  pl.BlockSpec((d, fc), lambda bi, c: (0, c)),
                  pl.BlockSpec((3, fc), lambda bi, c: (0, c)),
                  pl.BlockSpec((1, fc), lambda bi, c: (0, c)),
                  pl.BlockSpec((fc, d), lambda bi, c: (c, 0)),
                  pl.BlockSpec((1, d), lambda bi, c: (0, 0))],
        out_specs=pl.BlockSpec((1, s, d), lambda bi, c: (bi, 0, 0)),
        out_shape=jax.ShapeDtypeStruct((b, s, d), F32),
        scratch_shapes=[pltpu.VMEM((s, d), BF16)],
        compiler_params=_cparams("parallel", "arbitrary"),
        name="conv_ffn",
    )(x, g.reshape(1, d), w_gate, w_val, conv_w, conv_b.reshape(1, dff), w_down, final_g.reshape(1, d))


def _mla_prep_body(x_ref, g_ref, wdq_ref, qg_ref, wuq_ref, wuqs_ref, wdkv_ref, kvg_ref, wk_ref, wv_ref,
                   ct_ref, st_ref, q_out, k_out, v_out):
    hn = _rms(x_ref[...], g_ref[...]).astype(BF16)
    cq = _rms(_dot(hn, wdq_ref[...]), qg_ref[...]).astype(BF16)
    kva = _dot(hn, wdkv_ref[...])
    ckv = _rms(kva[:, :MLA_KV_RANK], kvg_ref[...]).astype(BF16)
    ct = ct_ref[...]
    st = st_ref[...]
    kr = kva[:, MLA_KV_RANK:MLA_KV_RANK + LANES]
    kr_sw = kva[:, MLA_KV_RANK + LANES:]
    roped = kr * ct[:, :LANES] + kr_sw * st[:, :LANES]
    roped2 = jnp.concatenate([roped, roped], axis=1)
    w2 = 2 * LANES
    for j in range(0, q_out.shape[1], w2):
        qh = _dot(cq, wuq_ref[:, j:j + w2])
        qs = _dot(cq, wuqs_ref[:, j:j + w2])
        q_out[:, j:j + w2] = (qh * ct + qs * st).astype(q_out.dtype)
        k_out[:, j:j + w2] = (_dot(ckv, wk_ref[:, j:j + w2]) + roped2).astype(k_out.dtype)
    v_out[...] = _dot(ckv, wv_ref[...]).astype(v_out.dtype)


def _mla_prep(x2, s, g, wdq, qg, wuq, wuqs, wdkv, kvg, wk, wv, ct, st, *, tm=512):
    t, d = x2.shape
    nblk = s // tm
    full = lambda a: pl.BlockSpec(a.shape, lambda i: (0, 0))
    hw = MLA_HEADS * LANES
    return pl.pallas_call(
        _mla_prep_body,
        grid=(t // tm,),
        in_specs=[pl.BlockSpec((tm, d), lambda i: (i, 0)), full(g), full(wdq), full(qg), full(wuq), full(wuqs),
                  full(wdkv), full(kvg), full(wk), full(wv),
                  pl.BlockSpec((tm, 2 * LANES), lambda i: (i % nblk, 0)),
                  pl.BlockSpec((tm, 2 * LANES), lambda i: (i % nblk, 0))],
        out_specs=[pl.BlockSpec((tm, hw), lambda i: (i, 0)),
                   pl.BlockSpec((tm, hw), lambda i: (i, 0)),
                   pl.BlockSpec((tm, MLA_HEADS * MLA_V), lambda i: (i, 0))],
        out_shape=[jax.ShapeDtypeStruct((t, hw), BF16),
                   jax.ShapeDtypeStruct((t, hw), BF16),
                   jax.ShapeDtypeStruct((t, MLA_HEADS * MLA_V), BF16)],
        compiler_params=_cparams("parallel"),
        name="mla_prep",
    )(x2, g, wdq, qg, wuq, wuqs, wdkv, kvg, wk, wv, ct, st)


def _mla_attn_body(q_ref, k_ref, v_ref, o_ref):
    scale = (MLA_NOPE + MLA_ROPE) ** -0.5
    v = v_ref[0]
    outs = []
    for hh in range(2):
        q = q_ref[0, :, hh * LANES:(hh + 1) * LANES]
        k = k_ref[0, :, hh * LANES:(hh + 1) * LANES]
        sc = lax.dot_general(q, k, NT_DIMS, preferred_element_type=F32) * scale
        e, l = _softmax_parts(sc)
        outs.append(_dot(e.astype(BF16), v) * (1.0 / l))
    o_ref[0] = jnp.where(_low_lanes(), outs[0], outs[1]).astype(o_ref.dtype)


def _mla_attention(q, k, v, *, tq=256):
    b, s, _ = q.shape
    pairs = MLA_HEADS // 2
    return pl.pallas_call(
        _mla_attn_body,
        grid=(b, pairs, s // tq),
        in_specs=[pl.BlockSpec((1, tq, 2 * LANES), lambda bi, p, qi: (bi, qi, p)),
                  pl.BlockSpec((1, s, 2 * LANES), lambda bi, p, qi: (bi, 0, p)),
                  pl.BlockSpec((1, s, LANES), lambda bi, p, qi: (bi, 0, p))],
        out_specs=pl.BlockSpec((1, tq, LANES), lambda bi, p, qi: (bi, qi, p)),
        out_shape=jax.ShapeDtypeStruct((b, s, pairs * LANES), BF16),
        compiler_params=_cparams("parallel", "parallel", "parallel"),
        name="mla_attn",
    )(q, k, v)


def _mla_weights(w_uq, w_dkv, w_ukv):
    f = MLA_NOPE + MLA_ROPE
    half = MLA_ROPE // 2
    r = w_uq.shape[0]
    uq = w_uq.reshape(r, MLA_HEADS, f)
    zq = jnp.zeros((r, MLA_HEADS, LANES - f), w_uq.dtype)
    wuq = jnp.concatenate([uq, zq], axis=-1).reshape(r, MLA_HEADS * LANES)
    wuqs = jnp.concatenate([jnp.zeros((r, MLA_HEADS, MLA_NOPE), w_uq.dtype),
                            uq[..., MLA_NOPE + half:], uq[..., MLA_NOPE:MLA_NOPE + half], zq],
                           axis=-1).reshape(r, MLA_HEADS * LANES)
    d = w_dkv.shape[0]
    kr = w_dkv[:, MLA_KV_RANK:]
    z64 = jnp.zeros((d, MLA_NOPE), w_dkv.dtype)
    z32 = jnp.zeros((d, LANES - f), w_dkv.dtype)
    wdkv = jnp.concatenate([w_dkv[:, :MLA_KV_RANK], z64, kr, z32,
                            z64, kr[:, half:], kr[:, :half], z32], axis=-1)
    c = w_ukv.shape[0]
    ukv = w_ukv.reshape(c, MLA_HEADS, MLA_NOPE + MLA_V)
    wk = jnp.concatenate([ukv[..., :MLA_NOPE], jnp.zeros((c, MLA_HEADS, LANES - MLA_NOPE), w_ukv.dtype)],
                         axis=-1).reshape(c, MLA_HEADS * LANES)
    wv = ukv[..., MLA_NOPE:].reshape(c, MLA_HEADS * MLA_V)
    return wuq.astype(BF16), wuqs.astype(BF16), wdkv.astype(BF16), wk.astype(BF16), wv.astype(BF16)


def _rope_lane_tables(s):
    inv = ROPE_THETA ** (-jnp.arange(0, MLA_ROPE, 2, dtype=F32) / MLA_ROPE)
    ang = jnp.arange(s, dtype=F32)[:, None] * inv[None, :]
    cos, sin = jnp.cos(ang), jnp.sin(ang)
    one = jnp.ones((s, MLA_NOPE), F32)
    z64 = jnp.zeros((s, MLA_NOPE), F32)
    z32 = jnp.zeros((s, LANES - MLA_NOPE - MLA_ROPE), F32)
    ct = jnp.concatenate([one, cos, cos, z32], axis=-1)
    st = jnp.concatenate([z64, -sin, sin, z32], axis=-1)
    return jnp.tile(ct, (1, 2)), jnp.tile(st, (1, 2))


def kernel(x, mix_norm_e, w_in_e, diff_lq1, diff_lk1, diff_lq2, diff_lk2, diff_subln_g, na_rpb, w_out_e,
           mix_norm_o, w_dq, q_norm_g, w_uq, w_dkv, kv_norm_g, w_ukv, w_o_mla, ffn_norm_g, w_ffn_gate,
           w_ffn_val, ffn_conv_w, ffn_conv_b, w_ffn_down, final_norm_g):
    b, s, d = x.shape
    t = b * s
    depth = ffn_norm_g.shape[0]
    for i in range(depth):
        j = i // 2
        x2 = x.reshape(t, d)
        if i % 2 == 0:
            lam_init = 0.8 - 0.6 * math.exp(-0.3 * i)
            proj = _norm_proj(x2, mix_norm_e[j], w_in_e[j].astype(BF16)).reshape(b, s, -1)
            lvec = jnp.stack([diff_lq1[j], diff_lk1[j], diff_lq2[j], diff_lk2[j]]).astype(F32)
            a = _diff_attention(proj, lvec, diff_subln_g[j], lam_init=lam_init)
            nb = _na_attention(proj, na_rpb[j], col0=3 * DIFF_HEADS)
            wo = w_out_e[j].astype(BF16)
            da = a.shape[-1]
            x2 = _proj_residual([a.reshape(t, da), nb.reshape(t, -1)], [wo[:da], wo[da:]], x2)
        else:
            wuq, wuqs, wdkv, wk, wv = _mla_weights(w_uq[j], w_dkv[j], w_ukv[j])
            ct, st = _rope_lane_tables(s)
            q, k, v = _mla_prep(x2, s, mix_norm_o[j].reshape(1, d), w_dq[j].astype(BF16),
                                q_norm_g[j].reshape(1, -1), wuq, wuqs, wdkv, kv_norm_g[j].reshape(1, -1),
                                wk, wv, ct, st)
            o = _mla_attention(q.reshape(b, s, -1), k.reshape(b, s, -1), v.reshape(b, s, -1))
            x2 = _proj_residual([o.reshape(t, -1)], [w_o_mla[j].astype(BF16)], x2)
        x = _conv_ffn(x2.reshape(b, s, d), ffn_norm_g[i], w_ffn_gate[i].astype(BF16), w_ffn_val[i].astype(BF16),
                      ffn_conv_w[i], ffn_conv_b[i], w_ffn_down[i].astype(BF16), final_norm_g,
                      final_norm=(i == depth - 1))
    return x
```

```python
import functools
import math

import numpy as np
import jax
import jax.numpy as jnp
from jax import lax
from jax.experimental import pallas as pl
from jax.experimental.pallas import tpu as pltpu

F32 = jnp.float32
BF16 = jnp.bfloat16

EPS = 1e-6
NEG_INF = -1e30
GRID_W = 64
HEAD_DIM = 64
DIFF_HEADS = 4
NA_HEADS = 8
NA_WIN_H = 8
NA_WIN_W = 16
MLA_HEADS = 16
MLA_NOPE = 64
MLA_ROPE = 32
MLA_V = 64
MLA_KV_RANK = 256
ROPE_THETA = 10000.0

LANES = 128
VMEM_LIMIT = 56 * 1024 * 1024
NT_DIMS = (((1,), (1,)), ((), ()))


def _cparams(*sem):
    return pltpu.CompilerParams(dimension_semantics=sem, vmem_limit_bytes=VMEM_LIMIT)


def _rms(xf, g):
    ms = jnp.mean(xf * xf, axis=-1, keepdims=True)
    return xf * lax.rsqrt(ms + EPS) * g


def _dot(a, b):
    return jnp.dot(a, b, preferred_element_type=F32)


LOG2E = math.log2(math.e)


def _low_lanes():
    return lax.broadcasted_iota(jnp.int32, (1, LANES), 1) < HEAD_DIM


def _norm_proj_body(x_ref, g_ref, w_ref, o_ref, *, n_chunk):
    hn = _rms(x_ref[...], g_ref[...]).astype(BF16)
    for j in range(0, o_ref.shape[1], n_chunk):
        o_ref[:, j:j + n_chunk] = _dot(hn, w_ref[:, j:j + n_chunk]).astype(o_ref.dtype)


def _norm_proj(x2, g, w, *, tm=512, n_chunk=768):
    t, d = x2.shape
    n = w.shape[1]
    return pl.pallas_call(
        functools.partial(_norm_proj_body, n_chunk=n_chunk),
        grid=(t // tm,),
        in_specs=[pl.BlockSpec((tm, d), lambda i: (i, 0)),
                  pl.BlockSpec((1, d), lambda i: (0, 0)),
                  pl.BlockSpec((d, n), lambda i: (0, 0))],
        out_specs=pl.BlockSpec((tm, n), lambda i: (i, 0)),
        out_shape=jax.ShapeDtypeStruct((t, n), BF16),
        compiler_params=_cparams("parallel"),
        name="norm_proj",
    )(x2, g.reshape(1, d), w)


def _exp_weights(t):
    return jnp.exp2(t - jnp.max(t, axis=-1, keepdims=True)).astype(BF16)


def _fill_ones_aug(vaug_ref, v_ref, n_blocks):
    ones = jnp.ones((v_ref.shape[1], LANES), BF16)
    for p in range(n_blocks):
        vaug_ref[:, 2 * p * LANES:(2 * p + 1) * LANES] = v_ref[0, :, p * LANES:(p + 1) * LANES]
        vaug_ref[:, (2 * p + 1) * LANES:(2 * p + 2) * LANES] = ones


def _diff_attn_body(lvec_ref, q_ref, k_ref, v_ref, g_ref, o_ref, vaug_ref, *, tq, lam_init):
    qi = pl.program_id(1)

    @pl.when(qi == 0)
    def _():
        _fill_ones_aug(vaug_ref, v_ref, DIFF_HEADS)

    s_len = k_ref.shape[1]
    lo = _low_lanes()
    row = qi * tq + lax.broadcasted_iota(jnp.int32, (tq, s_len), 0)
    col = lax.broadcasted_iota(jnp.int32, (tq, s_len), 1)
    dist = jnp.abs(row - col).astype(F32)
    scale = HEAD_DIM ** -0.5 * LOG2E
    lv = lvec_ref[...]
    lam = (jnp.exp(jnp.sum(lv[0:1] * lv[1:2], axis=-1, keepdims=True))
           - jnp.exp(jnp.sum(lv[2:3] * lv[3:4], axis=-1, keepdims=True)) + lam_init)
    for h in range(DIFF_HEADS):
        q = q_ref[0, :, h * LANES:(h + 1) * LANES]
        k = k_ref[0, :, h * LANES:(h + 1) * LANES]
        vaug = vaug_ref[:, 2 * h * LANES:(2 * h + 2) * LANES]
        zero = jnp.zeros_like(q)
        bias = (-LOG2E * 2.0 ** (-8.0 * (h + 1) / DIFF_HEADS)) * dist
        s1 = lax.dot_general(jnp.where(lo, q, zero), k, NT_DIMS, preferred_element_type=F32)
        s2 = lax.dot_general(jnp.where(lo, zero, q), k, NT_DIMS, preferred_element_type=F32)
        o1 = _dot(_exp_weights(s1 * scale + bias), vaug)
        o2 = _dot(_exp_weights(s2 * scale + bias), vaug)
        o = o1[:, :LANES] / o1[:, LANES:] - lam * (o2[:, :LANES] / o2[:, LANES:])
        o_ref[0, :, h * LANES:(h + 1) * LANES] = (_rms(o, g_ref[...]) * (1.0 - lam_init)).astype(o_ref.dtype)


def _diff_attention(proj, lvec, subln_g, *, lam_init, tq=256):
    b, s, _ = proj.shape
    w = DIFF_HEADS * LANES
    return pl.pallas_call(
        functools.partial(_diff_attn_body, tq=tq, lam_init=lam_init),
        grid=(b, s // tq),
        in_specs=[pl.BlockSpec((4, HEAD_DIM), lambda bi, qi: (0, 0)),
                  pl.BlockSpec((1, tq, w), lambda bi, qi: (bi, qi, 0)),
                  pl.BlockSpec((1, s, w), lambda bi, qi: (bi, 0, 1)),
                  pl.BlockSpec((1, s, w), lambda bi, qi: (bi, 0, 2)),
                  pl.BlockSpec((1, LANES), lambda bi, qi: (0, 0))],
        out_specs=pl.BlockSpec((1, tq, w), lambda bi, qi: (bi, qi, 0)),
        out_shape=jax.ShapeDtypeStruct((b, s, w), BF16),
        scratch_shapes=[pltpu.VMEM((s, 2 * w), BF16)],
        compiler_params=_cparams("parallel", "arbitrary"),
        name="diff_attn",
    )(lvec, proj, proj, proj, subln_g.reshape(1, LANES))


def _na_bias_table(rpb, kh):
    qc = np.arange(GRID_W)[:, None]
    kc = np.arange(GRID_W)[None, :]
    wstart = np.clip(qc - NA_WIN_W // 2, 0, GRID_W - NA_WIN_W)
    valid = (kc >= wstart) & (kc < wstart + NA_WIN_W)
    dc = kc - qc + NA_WIN_W - 1
    onehot = ((dc[None] == np.arange(2 * NA_WIN_W - 1)[:, None, None]) & valid[None]).astype(np.float32)
    toep = jnp.einsum("hrc,cqk->hrqk", rpb.astype(F32) * LOG2E, onehot, precision=lax.Precision.HIGHEST)
    toep = jnp.where(valid[None, None], toep, NEG_INF)
    per_d = [toep[:, NA_WIN_H - 1 - d:NA_WIN_H - 1 - d + kh].transpose(0, 2, 1, 3) for d in range(kh)]
    return jnp.stack(per_d, axis=1).reshape(rpb.shape[0], kh, GRID_W, kh * GRID_W)


def _na_body(q_ref, k_ref, v_ref, bias_ref, o_ref, vaug_ref, *, rows_n, kh, unroll):
    lo = _low_lanes()
    scale = HEAD_DIM ** -0.5 * LOG2E
    _fill_ones_aug(vaug_ref, v_ref, 1)

    def row(r, carry):
        rs = jnp.clip(r - kh // 2, 0, rows_n - kh)
        q = q_ref[0, pl.ds(pl.multiple_of(r * GRID_W, GRID_W), GRID_W), :]
        kwin = k_ref[0, pl.ds(pl.multiple_of(rs * GRID_W, GRID_W), kh * GRID_W), :]
        vwin = vaug_ref[pl.ds(pl.multiple_of(rs * GRID_W, GRID_W), kh * GRID_W), :]
        zero = jnp.zeros_like(q)
        q2 = jnp.concatenate([jnp.where(lo, q, zero), jnp.where(lo, zero, q)], axis=0)
        bias = bias_ref[:, r - rs].reshape(2 * GRID_W, kh * GRID_W)
        t = lax.dot_general(q2, kwin, NT_DIMS, preferred_element_type=F32) * scale + bias
        o2 = _dot(_exp_weights(t), vwin)
        o2 = o2[:, :LANES] / o2[:, LANES:]
        o_ref[0, pl.ds(pl.multiple_of(r * GRID_W, GRID_W), GRID_W), :] = (
            jnp.where(lo, o2[:GRID_W], o2[GRID_W:]).astype(o_ref.dtype))
        return carry

    lax.fori_loop(0, rows_n, row, 0, unroll=unroll)


def _na_attention(proj, rpb, *, col0):
    b, s, _ = proj.shape
    rows_n = s // GRID_W
    kh = min(NA_WIN_H, rows_n)
    pairs = NA_HEADS // 2
    bias = _na_bias_table(rpb, kh)
    return pl.pallas_call(
        functools.partial(_na_body, rows_n=rows_n, kh=kh, unroll=8),
        grid=(b, pairs),
        in_specs=[pl.BlockSpec((1, s, LANES), lambda bi, p: (bi, 0, col0 + p)),
                  pl.BlockSpec((1, s, LANES), lambda bi, p: (bi, 0, col0 + pairs + p)),
                  pl.BlockSpec((1, s, LANES), lambda bi, p: (bi, 0, col0 + 2 * pairs + p)),
                  pl.BlockSpec((2, kh, GRID_W, kh * GRID_W), lambda bi, p: (p, 0, 0, 0))],
        out_specs=pl.BlockSpec((1, s, LANES), lambda bi, p: (bi, 0, p)),
        out_shape=jax.ShapeDtypeStruct((b, s, pairs * LANES), BF16),
        scratch_shapes=[pltpu.VMEM((s, 2 * LANES), BF16)],
        compiler_params=_cparams("parallel", "parallel"),
        name="na_attn",
    )(proj, proj, proj, bias)


def _proj_res_body(*refs, n_in):
    x_ref, o_ref = refs[2 * n_in], refs[2 * n_in + 1]
    acc = x_ref[...]
    for a_ref, w_ref in zip(refs[:n_in], refs[n_in:2 * n_in]):
        acc = acc + _dot(a_ref[...], w_ref[...])
    o_ref[...] = acc


def _proj_residual(acts, ws, x2, *, tm=512):
    t, d = x2.shape
    n_in = len(acts)
    in_specs = ([pl.BlockSpec((tm, a.shape[1]), lambda i: (i, 0)) for a in acts]
                + [pl.BlockSpec(w.shape, lambda i: (0, 0)) for w in ws]
                + [pl.BlockSpec((tm, d), lambda i: (i, 0))])
    return pl.pallas_call(
        functools.partial(_proj_res_body, n_in=n_in),
        grid=(t // tm,),
        in_specs=in_specs,
        out_specs=pl.BlockSpec((tm, d), lambda i: (i, 0)),
        out_shape=jax.ShapeDtypeStruct((t, d), F32),
        compiler_params=_cparams("parallel"),
        name="proj_residual",
    )(*acts, *ws, x2)


def _ffn_body(x_ref, g_ref, wg_ref, wv_ref, cw_ref, cb_ref, wd_ref, fg_ref, o_ref, hn_ref, *, final_norm):
    c = pl.program_id(1)

    @pl.when(c == 0)
    def _():
        x = x_ref[0]
        hn_ref[...] = _rms(x, g_ref[...]).astype(BF16)
        o_ref[0] = x

    hn = hn_ref[...]
    a = _dot(hn, wg_ref[...])
    s_len = a.shape[0]
    t = lax.broadcasted_iota(jnp.int32, (s_len, 1), 0)
    a_prev = jnp.where(t == 0, 0.0, pltpu.roll(a, 1, 0))
    a_next = jnp.where(t == s_len - 1, 0.0, pltpu.roll(a, s_len - 1, 0))
    cw = cw_ref[...]
    a = cw[0:1] * a_prev + cw[1:2] * a + cw[2:3] * a_next + cb_ref[...]
    gate = 0.5 * a * (1.0 + lax.erf(a * np.float32(math.sqrt(0.5))))
    val = _dot(hn, wv_ref[...])
    o_ref[0] += _dot((gate * val).astype(BF16), wd_ref[...])

    if final_norm:
        @pl.when(c == pl.num_programs(1) - 1)
        def _():
            o_ref[0] = _rms(o_ref[0], fg_ref[...])


def _conv_ffn(x, g, w_gate, w_val, conv_w, conv_b, w_down, final_g, *, final_norm, fc=256):
    b, s, d = x.shape
    dff = w_gate.shape[1]
    return pl.pallas_call(
        functools.partial(_ffn_body, final_norm=final_norm),
        grid=(b, dff // fc),
        in_specs=[pl.BlockSpec((1, s, d), lambda bi, c: (bi, 0, 0)),
                  pl.BlockSpec((1, d), lambda bi, c: (0, 0)),
                  pl.BlockSpec((d, fc), lambda bi, c: (0, c)),
                  pl.BlockSpec((d, fc), lambda bi, c: (0, c)),
                  pl.BlockSpec((3, fc), lambda bi, c: (0, c)),
                  pl.BlockSpec((1, fc), lambda bi, c: (0, c)),
                  pl.BlockSpec((fc, d), lambda bi, c: (c, 0)),
                  pl.BlockSpec((1, d), lambda bi, c: (0, 0))],
        out_specs=pl.BlockSpec((1, s, d), lambda bi, c: (bi, 0, 0)),
        out_shape=jax.ShapeDtypeStruct((b, s, d), F32),
        scratch_shapes=[pltpu.VMEM((s, d), BF16)],
        compiler_params=_cparams("parallel", "arbitrary"),
        name="conv_ffn",
    )(x, g.reshape(1, d), w_gate, w_val, conv_w, conv_b.reshape(1, dff), w_down, final_g.reshape(1, d))


def _mla_prep_body(x_ref, g_ref, wdq_ref, qg_ref, wuq_ref, wuqs_ref, wdkv_ref, kvg_ref, wk_ref, wv_ref,
                   ct_ref, st_ref, q_out, k_out, v_out):
    hn = _rms(x_ref[...], g_ref[...]).astype(BF16)
    cq = _rms(_dot(hn, wdq_ref[...]), qg_ref[...]).astype(BF16)
    kva = _dot(hn, wdkv_ref[...])
    ckv = _rms(kva[:, :MLA_KV_RANK], kvg_ref[...]).astype(BF16)
    ct = ct_ref[...]
    st = st_ref[...]
    kr = kva[:, MLA_KV_RANK:MLA_KV_RANK + LANES]
    kr_sw = kva[:, MLA_KV_RANK + LANES:]
    roped = kr * ct[:, :LANES] + kr_sw * st[:, :LANES]
    roped2 = jnp.concatenate([roped, roped], axis=1)
    w2 = 2 * LANES
    for j in range(0, q_out.shape[1], w2):
        qh = _dot(cq, wuq_ref[:, j:j + w2])
        qs = _dot(cq, wuqs_ref[:, j:j + w2])
        q_out[:, j:j + w2] = (qh * ct + qs * st).astype(q_out.dtype)
        k_out[:, j:j + w2] = (_dot(ckv, wk_ref[:, j:j + w2]) + roped2).astype(k_out.dtype)
    v_out[...] = _dot(ckv, wv_ref[...]).astype(v_out.dtype)


def _mla_prep(x2, s, g, wdq, qg, wuq, wuqs, wdkv, kvg, wk, wv, ct, st, *, tm=512):
    t, d = x2.shape
    nblk = s // tm
    full = lambda a: pl.BlockSpec(a.shape, lambda i: (0, 0))
    hw = MLA_HEADS * LANES
    return pl.pallas_call(
        _mla_prep_body,
        grid=(t // tm,),
        in_specs=[pl.BlockSpec((tm, d), lambda i: (i, 0)), full(g), full(wdq), full(qg), full(wuq), full(wuqs),
                  full(wdkv), full(kvg), full(wk), full(wv),
                  pl.BlockSpec((tm, 2 * LANES), lambda i: (i % nblk, 0)),
                  pl.BlockSpec((tm, 2 * LANES), lambda i: (i % nblk, 0))],
        out_specs=[pl.BlockSpec((tm, hw), lambda i: (i, 0)),
                   pl.BlockSpec((tm, hw), lambda i: (i, 0)),
                   pl.BlockSpec((tm, MLA_HEADS * MLA_V), lambda i: (i, 0))],
        out_shape=[jax.ShapeDtypeStruct((t, hw), BF16),
                   jax.ShapeDtypeStruct((t, hw), BF16),
                   jax.ShapeDtypeStruct((t, MLA_HEADS * MLA_V), BF16)],
        compiler_params=_cparams("parallel"),
        name="mla_prep",
    )(x2, g, wdq, qg, wuq, wuqs, wdkv, kvg, wk, wv, ct, st)


def _mla_attn_body(q_ref, k_ref, v_ref, o_ref, vaug_ref, *, heads):
    scale = (MLA_NOPE + MLA_ROPE) ** -0.5 * LOG2E
    lo = _low_lanes()

    @pl.when(pl.program_id(2) == 0)
    def _():
        _fill_ones_aug(vaug_ref, v_ref, heads // 2)

    for p in range(heads // 2):
        vaug = vaug_ref[:, 2 * p * LANES:(2 * p + 2) * LANES]
        outs = []
        for hh in (2 * p, 2 * p + 1):
            q = q_ref[0, :, hh * LANES:(hh + 1) * LANES]
            k = k_ref[0, :, hh * LANES:(hh + 1) * LANES]
            e = _exp_weights(lax.dot_general(q, k, NT_DIMS, preferred_element_type=F32) * scale)
            o = _dot(e, vaug)
            outs.append(o[:, :LANES] / o[:, LANES:])
        o_ref[0, :, p * LANES:(p + 1) * LANES] = jnp.where(lo, outs[0], outs[1]).astype(o_ref.dtype)


def _mla_attention(q, k, v, *, tq=256, heads=8):
    b, s, _ = q.shape
    groups = MLA_HEADS // heads
    return pl.pallas_call(
        functools.partial(_mla_attn_body, heads=heads),
        grid=(b, groups, s // tq),
        in_specs=[pl.BlockSpec((1, tq, heads * LANES), lambda bi, g, qi: (bi, qi, g)),
                  pl.BlockSpec((1, s, heads * LANES), lambda bi, g, qi: (bi, 0, g)),
                  pl.BlockSpec((1, s, heads * MLA_V), lambda bi, g, qi: (bi, 0, g))],
        out_specs=pl.BlockSpec((1, tq, heads * MLA_V), lambda bi, g, qi: (bi, qi, g)),
        out_shape=jax.ShapeDtypeStruct((b, s, MLA_HEADS * MLA_V), BF16),
        scratch_shapes=[pltpu.VMEM((s, heads * LANES), BF16)],
        compiler_params=_cparams("parallel", "parallel", "arbitrary"),
        name="mla_attn",
    )(q, k, v)


def _mla_weights(w_uq, w_dkv, w_ukv):
    f = MLA_NOPE + MLA_ROPE
    half = MLA_ROPE // 2
    r = w_uq.shape[0]
    uq = w_uq.reshape(r, MLA_HEADS, f)
    zq = jnp.zeros((r, MLA_HEADS, LANES - f), w_uq.dtype)
    wuq = jnp.concatenate([uq, zq], axis=-1).reshape(r, MLA_HEADS * LANES)
    wuqs = jnp.concatenate([jnp.zeros((r, MLA_HEADS, MLA_NOPE), w_uq.dtype),
                            uq[..., MLA_NOPE + half:], uq[..., MLA_NOPE:MLA_NOPE + half], zq],
                           axis=-1).reshape(r, MLA_HEADS * LANES)
    d = w_dkv.shape[0]
    kr = w_dkv[:, MLA_KV_RANK:]
    z64 = jnp.zeros((d, MLA_NOPE), w_dkv.dtype)
    z32 = jnp.zeros((d, LANES - f), w_dkv.dtype)
    wdkv = jnp.concatenate([w_dkv[:, :MLA_KV_RANK], z64, kr, z32,
                            z64, kr[:, half:], kr[:, :half], z32], axis=-1)
    c = w_ukv.shape[0]
    ukv = w_ukv.reshape(c, MLA_HEADS, MLA_NOPE + MLA_V)
    wk = jnp.concatenate([ukv[..., :MLA_NOPE], jnp.zeros((c, MLA_HEADS, LANES - MLA_NOPE), w_ukv.dtype)],
                         axis=-1).reshape(c, MLA_HEADS * LANES)
    wv = ukv[..., MLA_NOPE:].reshape(c, MLA_HEADS * MLA_V)
    return wuq.astype(BF16), wuqs.astype(BF16), wdkv.astype(BF16), wk.astype(BF16), wv.astype(BF16)


def _rope_lane_tables(s):
    inv = ROPE_THETA ** (-jnp.arange(0, MLA_ROPE, 2, dtype=F32) / MLA_ROPE)
    ang = jnp.arange(s, dtype=F32)[:, None] * inv[None, :]
    cos, sin = jnp.cos(ang), jnp.sin(ang)
    one = jnp.ones((s, MLA_NOPE), F32)
    z64 = jnp.zeros((s, MLA_NOPE), F32)
    z32 = jnp.zeros((s, LANES - MLA_NOPE - MLA_ROPE), F32)
    ct = jnp.concatenate([one, cos, cos, z32], axis=-1)
    st = jnp.concatenate([z64, -sin, sin, z32], axis=-1)
    return jnp.tile(ct, (1, 2)), jnp.tile(st, (1, 2))


def kernel(x, mix_norm_e, w_in_e, diff_lq1, diff_lk1, diff_lq2, diff_lk2, diff_subln_g, na_rpb, w_out_e,
           mix_norm_o, w_dq, q_norm_g, w_uq, w_dkv, kv_norm_g, w_ukv, w_o_mla, ffn_norm_g, w_ffn_gate,
           w_ffn_val, ffn_conv_w, ffn_conv_b, w_ffn_down, final_norm_g):
    b, s, d = x.shape
    t = b * s
    depth = ffn_norm_g.shape[0]
    for i in range(depth):
        j = i // 2
        x2 = x.reshape(t, d)
        if i % 2 == 0:
            lam_init = 0.8 - 0.6 * math.exp(-0.3 * i)
            proj = _norm_proj(x2, mix_norm_e[j], w_in_e[j].astype(BF16)).reshape(b, s, -1)
            lvec = jnp.stack([diff_lq1[j], diff_lk1[j], diff_lq2[j], diff_lk2[j]]).astype(F32)
            a = _diff_attention(proj, lvec, diff_subln_g[j], lam_init=lam_init)
            nb = _na_attention(proj, na_rpb[j], col0=3 * DIFF_HEADS)
            wo = w_out_e[j].astype(BF16)
            da = a.shape[-1]
            x2 = _proj_residual([a.reshape(t, da), nb.reshape(t, -1)], [wo[:da], wo[da:]], x2)
        else:
            wuq, wuqs, wdkv, wk, wv = _mla_weights(w_uq[j], w_dkv[j], w_ukv[j])
            ct, st = _rope_lane_tables(s)
            q, k, v = _mla_prep(x2, s, mix_norm_o[j].reshape(1, d), w_dq[j].astype(BF16),
                                q_norm_g[j].reshape(1, -1), wuq, wuqs, wdkv, kv_norm_g[j].reshape(1, -1),
                                wk, wv, ct, st)
            o = _mla_attention(q.reshape(b, s, -1), k.reshape(b, s, -1), v.reshape(b, s, -1))
            x2 = _proj_residual([o.reshape(t, -1)], [w_o_mla[j].astype(BF16)], x2)
        x = _conv_ffn(x2.reshape(b, s, d), ffn_norm_g[i], w_ffn_gate[i].astype(BF16), w_ffn_val[i].astype(BF16),
                      ffn_conv_w[i], ffn_conv_b[i], w_ffn_down[i].astype(BF16), final_norm_g,
                      final_norm=(i == depth - 1))
    return x
```

```python
import functools
import math

import numpy as np
import jax
import jax.numpy as jnp
from jax import lax
from jax.experimental import pallas as pl
from jax.experimental.pallas import tpu as pltpu

F32 = jnp.float32
BF16 = jnp.bfloat16

EPS = 1e-6
NEG_INF = -1e30
GRID_W = 64
HEAD_DIM = 64
DIFF_HEADS = 4
NA_HEADS = 8
NA_WIN_H = 8
NA_WIN_W = 16
MLA_HEADS = 16
MLA_NOPE = 64
MLA_ROPE = 32
MLA_V = 64
MLA_KV_RANK = 256
ROPE_THETA = 10000.0

LANES = 128
VMEM_LIMIT = 56 * 1024 * 1024
NT_DIMS = (((1,), (1,)), ((), ()))
LOG2E = math.log2(math.e)
SQRT_HALF = np.float32(math.sqrt(0.5))


def _cparams(*sem):
    return pltpu.CompilerParams(dimension_semantics=sem, vmem_limit_bytes=VMEM_LIMIT)


def _rms(xf, g):
    ms = jnp.mean(xf * xf, axis=-1, keepdims=True)
    return xf * lax.rsqrt(ms + EPS) * g


def _dot(a, b):
    return jnp.dot(a, b, preferred_element_type=F32)


def _low_lanes():
    return lax.broadcasted_iota(jnp.int32, (1, LANES), 1) < HEAD_DIM


def _exp_weights(t):
    return jnp.exp2(t - jnp.max(t, axis=-1, keepdims=True)).astype(BF16)


def _fill_ones_aug(vaug_ref, v_ref, n_blocks):
    ones = jnp.ones((v_ref.shape[1], LANES), BF16)
    for p in range(n_blocks):
        vaug_ref[:, 2 * p * LANES:(2 * p + 1) * LANES] = v_ref[0, :, p * LANES:(p + 1) * LANES]
        vaug_ref[:, (2 * p + 1) * LANES:(2 * p + 2) * LANES] = ones


def _norm_proj_body(x_ref, g_ref, w_ref, o_ref, *, n_chunk):
    hn = _rms(x_ref[...], g_ref[...]).astype(BF16)
    for j in range(0, o_ref.shape[1], n_chunk):
        o_ref[:, j:j + n_chunk] = _dot(hn, w_ref[:, j:j + n_chunk]).astype(o_ref.dtype)


def _norm_proj(x2, g, w, *, tm=512, n_chunk=768):
    t, d = x2.shape
    n = w.shape[1]
    return pl.pallas_call(
        functools.partial(_norm_proj_body, n_chunk=n_chunk),
        grid=(t // tm,),
        in_specs=[pl.BlockSpec((tm, d), lambda i: (i, 0)),
                  pl.BlockSpec((1, d), lambda i: (0, 0)),
                  pl.BlockSpec((d, n), lambda i: (0, 0))],
        out_specs=pl.BlockSpec((tm, n), lambda i: (i, 0)),
        out_shape=jax.ShapeDtypeStruct((t, n), BF16),
        compiler_params=_cparams("parallel"),
        name="norm_proj",
    )(x2, g.reshape(1, d), w)


def _diff_attn_body(lvec_ref, q_ref, k_ref, v_ref, g_ref, o_ref, vaug_ref, *, tq, lam_init):
    qi = pl.program_id(1)

    @pl.when(qi == 0)
    def _():
        _fill_ones_aug(vaug_ref, v_ref, DIFF_HEADS)

    s_len = k_ref.shape[1]
    lo = _low_lanes()
    row = qi * tq + lax.broadcasted_iota(jnp.int32, (tq, s_len), 0)
    col = lax.broadcasted_iota(jnp.int32, (tq, s_len), 1)
    dist = jnp.abs(row - col).astype(F32)
    scale = HEAD_DIM ** -0.5 * LOG2E
    lv = lvec_ref[...]
    lam = (jnp.exp(jnp.sum(lv[0:1] * lv[1:2], axis=-1, keepdims=True))
           - jnp.exp(jnp.sum(lv[2:3] * lv[3:4], axis=-1, keepdims=True)) + lam_init)
    for h in range(DIFF_HEADS):
        q = q_ref[0, :, h * LANES:(h + 1) * LANES]
        k = k_ref[0, :, h * LANES:(h + 1) * LANES]
        vaug = vaug_ref[:, 2 * h * LANES:(2 * h + 2) * LANES]
        zero = jnp.zeros_like(q)
        bias = (-LOG2E * 2.0 ** (-8.0 * (h + 1) / DIFF_HEADS)) * dist
        s1 = lax.dot_general(jnp.where(lo, q, zero), k, NT_DIMS, preferred_element_type=F32)
        s2 = lax.dot_general(jnp.where(lo, zero, q), k, NT_DIMS, preferred_element_type=F32)
        o1 = _dot(_exp_weights(s1 * scale + bias), vaug)
        o2 = _dot(_exp_weights(s2 * scale + bias), vaug)
        o = o1[:, :LANES] / o1[:, LANES:] - lam * (o2[:, :LANES] / o2[:, LANES:])
        o_ref[0, :, h * LANES:(h + 1) * LANES] = (_rms(o, g_ref[...]) * (1.0 - lam_init)).astype(o_ref.dtype)


def _diff_attention(proj, lvec, subln_g, *, lam_init, tq=256):
    b, s, _ = proj.shape
    w = DIFF_HEADS * LANES
    return pl.pallas_call(
        functools.partial(_diff_attn_body, tq=tq, lam_init=lam_init),
        grid=(b, s // tq),
        in_specs=[pl.BlockSpec((4, HEAD_DIM), lambda bi, qi: (0, 0)),
                  pl.BlockSpec((1, tq, w), lambda bi, qi: (bi, qi, 0)),
                  pl.BlockSpec((1, s, w), lambda bi, qi: (bi, 0, 1)),
                  pl.BlockSpec((1, s, w), lambda bi, qi: (bi, 0, 2)),
                  pl.BlockSpec((1, LANES), lambda bi, qi: (0, 0))],
        out_specs=pl.BlockSpec((1, tq, w), lambda bi, qi: (bi, qi, 0)),
        out_shape=jax.ShapeDtypeStruct((b, s, w), BF16),
        scratch_shapes=[pltpu.VMEM((s, 2 * w), BF16)],
        compiler_params=_cparams("parallel", "arbitrary"),
        name="diff_attn",
    )(lvec, proj, proj, proj, subln_g.reshape(1, LANES))


def _na_bias_table(rpb, kh):
    qc = np.arange(GRID_W)[:, None]
    kc = np.arange(GRID_W)[None, :]
    wstart = np.clip(qc - NA_WIN_W // 2, 0, GRID_W - NA_WIN_W)
    valid = (kc >= wstart) & (kc < wstart + NA_WIN_W)
    dc = kc - qc + NA_WIN_W - 1
    onehot = ((dc[None] == np.arange(2 * NA_WIN_W - 1)[:, None, None]) & valid[None]).astype(np.float32)
    toep = jnp.einsum("hrc,cqk->hrqk", rpb.astype(F32) * LOG2E, onehot, precision=lax.Precision.HIGHEST)
    toep = jnp.where(valid[None, None], toep, NEG_INF)
    per_d = [toep[:, NA_WIN_H - 1 - d:NA_WIN_H - 1 - d + kh].transpose(0, 2, 1, 3) for d in range(kh)]
    return jnp.stack(per_d, axis=1).reshape(rpb.shape[0], kh, GRID_W, kh * GRID_W)


def _na_body(q_ref, k_ref, v_ref, bias_ref, o_ref, vaug_ref, *, rows_n, kh, unroll):
    lo = _low_lanes()
    scale = HEAD_DIM ** -0.5 * LOG2E
    _fill_ones_aug(vaug_ref, v_ref, 1)

    def row(r, carry):
        rs = jnp.clip(r - kh // 2, 0, rows_n - kh)
        q = q_ref[0, pl.ds(pl.multiple_of(r * GRID_W, GRID_W), GRID_W), :]
        kwin = k_ref[0, pl.ds(pl.multiple_of(rs * GRID_W, GRID_W), kh * GRID_W), :]
        vwin = vaug_ref[pl.ds(pl.multiple_of(rs * GRID_W, GRID_W), kh * GRID_W), :]
        zero = jnp.zeros_like(q)
        q2 = jnp.concatenate([jnp.where(lo, q, zero), jnp.where(lo, zero, q)], axis=0)
        bias = bias_ref[:, r - rs].reshape(2 * GRID_W, kh * GRID_W)
        t = lax.dot_general(q2, kwin, NT_DIMS, preferred_element_type=F32) * scale + bias
        o2 = _dot(_exp_weights(t), vwin)
        o2 = o2[:, :LANES] / o2[:, LANES:]
        o_ref[0, pl.ds(pl.multiple_of(r * GRID_W, GRID_W), GRID_W), :] = (
            jnp.where(lo, o2[:GRID_W], o2[GRID_W:]).astype(o_ref.dtype))
        return carry

    lax.fori_loop(0, rows_n, row, 0, unroll=unroll)


def _na_attention(proj, rpb, *, col0):
    b, s, _ = proj.shape
    rows_n = s // GRID_W
    kh = min(NA_WIN_H, rows_n)
    pairs = NA_HEADS // 2
    bias = _na_bias_table(rpb, kh)
    return pl.pallas_call(
        functools.partial(_na_body, rows_n=rows_n, kh=kh, unroll=8),
        grid=(b, pairs),
        in_specs=[pl.BlockSpec((1, s, LANES), lambda bi, p: (bi, 0, col0 + p)),
                  pl.BlockSpec((1, s, LANES), lambda bi, p: (bi, 0, col0 + pairs + p)),
                  pl.BlockSpec((1, s, LANES), lambda bi, p: (bi, 0, col0 + 2 * pairs + p)),
                  pl.BlockSpec((2, kh, GRID_W, kh * GRID_W), lambda bi, p: (p, 0, 0, 0))],
        out_specs=pl.BlockSpec((1, s, LANES), lambda bi, p: (bi, 0, p)),
        out_shape=jax.ShapeDtypeStruct((b, s, pairs * LANES), BF16),
        scratch_shapes=[pltpu.VMEM((s, 2 * LANES), BF16)],
        compiler_params=_cparams("parallel", "parallel"),
        name="na_attn",
    )(proj, proj, proj, bias)


HALO = 16


def _mix_ffn_body(*refs, n_act, tiles_per_seq, final_norm):
    acts = refs[:3 * n_act]
    wos = refs[3 * n_act:4 * n_act]
    x_ref, xp_ref, xn_ref, g_ref, wg_ref, wv_ref, cw_ref, cb_ref, wd_ref, fg_ref, o_ref = refs[4 * n_act:]
    pos = pl.program_id(0) % tiles_per_seq
    tm = x_ref.shape[0]
    xe = jnp.concatenate([xp_ref[...], x_ref[...], xn_ref[...]], axis=0)
    for k in range(n_act):
        a_ref, ap_ref, an_ref = acts[3 * k:3 * k + 3]
        xe = xe + _dot(jnp.concatenate([ap_ref[...], a_ref[...], an_ref[...]], axis=0), wos[k][...])
    row = lax.broadcasted_iota(jnp.int32, (tm + 2 * HALO, 1), 0)
    first_kept = jnp.where(pos == 0, HALO, 0)
    end_kept = jnp.where(pos == tiles_per_seq - 1, HALO + tm, tm + 2 * HALO)
    xe = jnp.where((row >= first_kept) & (row < end_kept), xe, 0.0)
    y = _rms(xe, g_ref[...])
    a_ext = _dot(y.astype(BF16), wg_ref[...])
    cw = cw_ref[...]
    a = (cw[0:1] * a_ext[HALO - 1:HALO - 1 + tm] + cw[1:2] * a_ext[HALO:HALO + tm]
         + cw[2:3] * a_ext[HALO + 1:HALO + 1 + tm] + cb_ref[...])
    gate = 0.5 * a * (1.0 + lax.erf(a * SQRT_HALF))
    val = _dot(y[HALO:HALO + tm].astype(BF16), wv_ref[...])
    out = xe[HALO:HALO + tm] + _dot((gate * val).astype(BF16), wd_ref[...])
    o_ref[...] = _rms(out, fg_ref[...]) if final_norm else out


def _mix_ffn(acts, wos, x2, s, g, w_gate, w_val, conv_w, conv_b, w_down, final_g, *, final_norm, tm=512):
    t, d = x2.shape
    dff = w_gate.shape[1]
    hb = tm // HALO
    last = t // HALO - 1

    def tiles(width):
        return [pl.BlockSpec((tm, width), lambda i: (i, 0)),
                pl.BlockSpec((HALO, width), lambda i: (jnp.maximum(i * hb - 1, 0), 0)),
                pl.BlockSpec((HALO, width), lambda i: (jnp.minimum((i + 1) * hb, last), 0))]

    consts = (*wos, g.reshape(1, d), w_gate, w_val, conv_w, conv_b.reshape(1, dff), w_down, final_g.reshape(1, d))
    const_specs = [pl.BlockSpec(c.shape, lambda i: (0, 0), pipeline_mode=pl.Buffered(1)) for c in consts]
    act_args = [a for act in acts for a in (act, act, act)]
    act_specs = [sp for act in acts for sp in tiles(act.shape[1])]
    return pl.pallas_call(
        functools.partial(_mix_ffn_body, n_act=len(acts), tiles_per_seq=s // tm, final_norm=final_norm),
        grid=(t // tm,),
        in_specs=act_specs + const_specs[:len(wos)] + tiles(d) + const_specs[len(wos):],
        out_specs=pl.BlockSpec((tm, d), lambda i: (i, 0)),
        out_shape=jax.ShapeDtypeStruct((t, d), F32),
        compiler_params=_cparams("parallel"),
        name="mix_ffn",
    )(*act_args, *consts[:len(wos)], x2, x2, x2, *consts[len(wos):])


def _mla_prep_body(x_ref, g_ref, wdq_ref, qg_ref, wuq_ref, wuqs_ref, wdkv_ref, kvg_ref, wk_ref, wv_ref,
                   ct_ref, st_ref, q_out, k_out, v_out):
    hn = _rms(x_ref[...], g_ref[...]).astype(BF16)
    cq = _rms(_dot(hn, wdq_ref[...]), qg_ref[...]).astype(BF16)
    kva = _dot(hn, wdkv_ref[...])
    ckv = _rms(kva[:, :MLA_KV_RANK], kvg_ref[...]).astype(BF16)
    ct = ct_ref[...]
    st = st_ref[...]
    kr = kva[:, MLA_KV_RANK:MLA_KV_RANK + LANES]
    kr_sw = kva[:, MLA_KV_RANK + LANES:]
    roped = kr * ct[:, :LANES] + kr_sw * st[:, :LANES]
    roped2 = jnp.concatenate([roped, roped], axis=1)
    w2 = 2 * LANES
    for j in range(0, q_out.shape[1], w2):
        qh = _dot(cq, wuq_ref[:, j:j + w2])
        qs = _dot(cq, wuqs_ref[:, j:j + w2])
        q_out[:, j:j + w2] = (qh * ct + qs * st).astype(q_out.dtype)
        k_out[:, j:j + w2] = (_dot(ckv, wk_ref[:, j:j + w2]) + roped2).astype(k_out.dtype)
    v_out[...] = _dot(ckv, wv_ref[...]).astype(v_out.dtype)


def _mla_prep(x2, s, g, wdq, qg, wuq, wuqs, wdkv, kvg, wk, wv, ct, st, *, tm=512):
    t, d = x2.shape
    nblk = s // tm
    full = lambda a: pl.BlockSpec(a.shape, lambda i: (0, 0))
    hw = MLA_HEADS * LANES
    return pl.pallas_call(
        _mla_prep_body,
        grid=(t // tm,),
        in_specs=[pl.BlockSpec((tm, d), lambda i: (i, 0)), full(g), full(wdq), full(qg), full(wuq), full(wuqs),
                  full(wdkv), full(kvg), full(wk), full(wv),
                  pl.BlockSpec((tm, 2 * LANES), lambda i: (i % nblk, 0)),
                  pl.BlockSpec((tm, 2 * LANES), lambda i: (i % nblk, 0))],
        out_specs=[pl.BlockSpec((tm, hw), lambda i: (i, 0)),
                   pl.BlockSpec((tm, hw), lambda i: (i, 0)),
                   pl.BlockSpec((tm, MLA_HEADS * MLA_V), lambda i: (i, 0))],
        out_shape=[jax.ShapeDtypeStruct((t, hw), BF16),
                   jax.ShapeDtypeStruct((t, hw), BF16),
                   jax.ShapeDtypeStruct((t, MLA_HEADS * MLA_V), BF16)],
        compiler_params=_cparams("parallel"),
        name="mla_prep",
    )(x2, g, wdq, qg, wuq, wuqs, wdkv, kvg, wk, wv, ct, st)


def _mla_attn_body(q_ref, k_ref, v_ref, o_ref, vaug_ref, t_ref, *, heads):
    scale = (MLA_NOPE + MLA_ROPE) ** -0.5 * LOG2E
    lo = _low_lanes()

    @pl.when(pl.program_id(2) == 0)
    def _():
        _fill_ones_aug(vaug_ref, v_ref, heads // 2)

    def scores(hh):
        q = q_ref[0, :, hh * LANES:(hh + 1) * LANES]
        k = k_ref[0, :, hh * LANES:(hh + 1) * LANES]
        t_ref[hh % 2] = lax.dot_general(q, k, NT_DIMS, preferred_element_type=F32) * scale

    scores(0)
    outs = []
    for hh in range(heads):
        if hh + 1 < heads:
            scores(hh + 1)
        p = hh // 2
        o = _dot(_exp_weights(t_ref[hh % 2]), vaug_ref[:, 2 * p * LANES:(2 * p + 2) * LANES])
        outs.append(o[:, :LANES] / o[:, LANES:])
        if hh % 2 == 1:
            o_ref[0, :, p * LANES:(p + 1) * LANES] = jnp.where(lo, outs[0], outs[1]).astype(o_ref.dtype)
            outs = []


def _mla_attention(q, k, v, *, tq=256, heads=8):
    b, s, _ = q.shape
    groups = MLA_HEADS // heads
    return pl.pallas_call(
        functools.partial(_mla_attn_body, heads=heads),
        grid=(b, groups, s // tq),
        in_specs=[pl.BlockSpec((1, tq, heads * LANES), lambda bi, g, qi: (bi, qi, g)),
                  pl.BlockSpec((1, s, heads * LANES), lambda bi, g, qi: (bi, 0, g)),
                  pl.BlockSpec((1, s, heads * MLA_V), lambda bi, g, qi: (bi, 0, g))],
        out_specs=pl.BlockSpec((1, tq, heads * MLA_V), lambda bi, g, qi: (bi, qi, g)),
        out_shape=jax.ShapeDtypeStruct((b, s, MLA_HEADS * MLA_V), BF16),
        scratch_shapes=[pltpu.VMEM((s, heads * LANES), BF16), pltpu.VMEM((2, tq, s), F32)],
        compiler_params=_cparams("parallel", "parallel", "arbitrary"),
        name="mla_attn",
    )(q, k, v)


def _mla_weights(w_uq, w_dkv, w_ukv):
    f = MLA_NOPE + MLA_ROPE
    half = MLA_ROPE // 2
    r = w_uq.shape[0]
    uq = w_uq.reshape(r, MLA_HEADS, f)
    zq = jnp.zeros((r, MLA_HEADS, LANES - f), w_uq.dtype)
    wuq = jnp.concatenate([uq, zq], axis=-1).reshape(r, MLA_HEADS * LANES)
    wuqs = jnp.concatenate([jnp.zeros((r, MLA_HEADS, MLA_NOPE), w_uq.dtype),
                            uq[..., MLA_NOPE + half:], uq[..., MLA_NOPE:MLA_NOPE + half], zq],
                           axis=-1).reshape(r, MLA_HEADS * LANES)
    d = w_dkv.shape[0]
    kr = w_dkv[:, MLA_KV_RANK:]
    z64 = jnp.zeros((d, MLA_NOPE), w_dkv.dtype)
    z32 = jnp.zeros((d, LANES - f), w_dkv.dtype)
    wdkv = jnp.concatenate([w_dkv[:, :MLA_KV_RANK], z64, kr, z32,
                            z64, kr[:, half:], kr[:, :half], z32], axis=-1)
    c = w_ukv.shape[0]
    ukv = w_ukv.reshape(c, MLA_HEADS, MLA_NOPE + MLA_V)
    wk = jnp.concatenate([ukv[..., :MLA_NOPE], jnp.zeros((c, MLA_HEADS, LANES - MLA_NOPE), w_ukv.dtype)],
                         axis=-1).reshape(c, MLA_HEADS * LANES)
    wv = ukv[..., MLA_NOPE:].reshape(c, MLA_HEADS * MLA_V)
    return wuq.astype(BF16), wuqs.astype(BF16), wdkv.astype(BF16), wk.astype(BF16), wv.astype(BF16)


def _rope_lane_tables(s):
    inv = ROPE_THETA ** (-jnp.arange(0, MLA_ROPE, 2, dtype=F32) / MLA_ROPE)
    ang = jnp.arange(s, dtype=F32)[:, None] * inv[None, :]
    cos, sin = jnp.cos(ang), jnp.sin(ang)
    one = jnp.ones((s, MLA_NOPE), F32)
    z64 = jnp.zeros((s, MLA_NOPE), F32)
    z32 = jnp.zeros((s, LANES - MLA_NOPE - MLA_ROPE), F32)
    ct = jnp.concatenate([one, cos, cos, z32], axis=-1)
    st = jnp.concatenate([z64, -sin, sin, z32], axis=-1)
    return jnp.tile(ct, (1, 2)), jnp.tile(st, (1, 2))


def kernel(x, mix_norm_e, w_in_e, diff_lq1, diff_lk1, diff_lq2, diff_lk2, diff_subln_g, na_rpb, w_out_e,
           mix_norm_o, w_dq, q_norm_g, w_uq, w_dkv, kv_norm_g, w_ukv, w_o_mla, ffn_norm_g, w_ffn_gate,
           w_ffn_val, ffn_conv_w, ffn_conv_b, w_ffn_down, final_norm_g):
    b, s, d = x.shape
    t = b * s
    depth = ffn_norm_g.shape[0]
    for i in range(depth):
        j = i // 2
        x2 = x.reshape(t, d)
        if i % 2 == 0:
            lam_init = 0.8 - 0.6 * math.exp(-0.3 * i)
            proj = _norm_proj(x2, mix_norm_e[j], w_in_e[j].astype(BF16)).reshape(b, s, -1)
            lvec = jnp.stack([diff_lq1[j], diff_lk1[j], diff_lq2[j], diff_lk2[j]]).astype(F32)
            a = _diff_attention(proj, lvec, diff_subln_g[j], lam_init=lam_init)
            nb = _na_attention(proj, na_rpb[j], col0=3 * DIFF_HEADS)
            wo = w_out_e[j].astype(BF16)
            da = a.shape[-1]
            acts, wos = [a.reshape(t, da), nb.reshape(t, -1)], [wo[:da], wo[da:]]
        else:
            wuq, wuqs, wdkv, wk, wv = _mla_weights(w_uq[j], w_dkv[j], w_ukv[j])
            ct, st = _rope_lane_tables(s)
            q, k, v = _mla_prep(x2, s, mix_norm_o[j].reshape(1, d), w_dq[j].astype(BF16),
                                q_norm_g[j].reshape(1, -1), wuq, wuqs, wdkv, kv_norm_g[j].reshape(1, -1),
                                wk, wv, ct, st)
            o = _mla_attention(q.reshape(b, s, -1), k.reshape(b, s, -1), v.reshape(b, s, -1))
            acts, wos = [o.reshape(t, -1)], [w_o_mla[j].astype(BF16)]
        x = _mix_ffn(acts, wos, x2, s, ffn_norm_g[i], w_ffn_gate[i].astype(BF16), w_ffn_val[i].astype(BF16),
                     ffn_conv_w[i], ffn_conv_b[i], w_ffn_down[i].astype(BF16), final_norm_g,
                     final_norm=(i == depth - 1)).reshape(b, s, d)
    return x
```

```python
import functools
import math

import numpy as np
import jax
import jax.numpy as jnp
from jax import lax
from jax.experimental import pallas as pl
from jax.experimental.pallas import tpu as pltpu

F32 = jnp.float32
BF16 = jnp.bfloat16

EPS = 1e-6
NEG_INF = -1e30
GRID_W = 64
HEAD_DIM = 64
DIFF_HEADS = 4
NA_HEADS = 8
NA_WIN_H = 8
NA_WIN_W = 16
MLA_HEADS = 16
MLA_NOPE = 64
MLA_ROPE = 32
MLA_V = 64
MLA_KV_RANK = 256
ROPE_THETA = 10000.0

LANES = 128
VMEM_LIMIT = 56 * 1024 * 1024
NT_DIMS = (((1,), (1,)), ((), ()))
LOG2E = math.log2(math.e)
SQRT_HALF = np.float32(math.sqrt(0.5))
SOFTMAX_SCALE = HEAD_DIM ** -0.5 * LOG2E
MLA_SOFTMAX_SCALE = (MLA_NOPE + MLA_ROPE) ** -0.5 * LOG2E


def _cparams(*sem):
    return pltpu.CompilerParams(dimension_semantics=sem, vmem_limit_bytes=VMEM_LIMIT)


def _rms(xf, g):
    ms = jnp.mean(xf * xf, axis=-1, keepdims=True)
    return xf * lax.rsqrt(ms + EPS) * g


def _dot(a, b):
    return jnp.dot(a, b, preferred_element_type=F32)


def _low_lanes():
    return lax.broadcasted_iota(jnp.int32, (1, LANES), 1) < HEAD_DIM


def _exp_weights(t):
    return jnp.exp2(t - jnp.max(t, axis=-1, keepdims=True)).astype(BF16)


def _fill_ones_aug(vaug_ref, v_ref, n_blocks):
    ones = jnp.ones((v_ref.shape[1], LANES), BF16)
    for p in range(n_blocks):
        vaug_ref[:, 2 * p * LANES:(2 * p + 1) * LANES] = v_ref[0, :, p * LANES:(p + 1) * LANES]
        vaug_ref[:, (2 * p + 1) * LANES:(2 * p + 2) * LANES] = ones


def _norm_proj_body(x_ref, g_ref, w_ref, cs_ref, o_ref, *, n_chunk):
    hn = _rms(x_ref[...], g_ref[...]).astype(BF16)
    for j in range(0, o_ref.shape[1], n_chunk):
        o_ref[:, j:j + n_chunk] = (_dot(hn, w_ref[:, j:j + n_chunk]) * cs_ref[:, j:j + n_chunk]).astype(o_ref.dtype)


def _norm_proj(x2, g, w, col_scale, *, tm=512, n_chunk=768):
    t, d = x2.shape
    n = w.shape[1]
    return pl.pallas_call(
        functools.partial(_norm_proj_body, n_chunk=n_chunk),
        grid=(t // tm,),
        in_specs=[pl.BlockSpec((tm, d), lambda i: (i, 0)),
                  pl.BlockSpec((1, d), lambda i: (0, 0)),
                  pl.BlockSpec((d, n), lambda i: (0, 0)),
                  pl.BlockSpec((1, n), lambda i: (0, 0))],
        out_specs=pl.BlockSpec((tm, n), lambda i: (i, 0)),
        out_shape=jax.ShapeDtypeStruct((t, n), BF16),
        compiler_params=_cparams("parallel"),
        name="norm_proj",
    )(x2, g.reshape(1, d), w, col_scale)


def _diff_attn_body(lvec_ref, q_ref, k_ref, v_ref, g_ref, o_ref, vaug_ref, t_ref, *, tq, lam_init):
    qi = pl.program_id(1)

    @pl.when(qi == 0)
    def _():
        _fill_ones_aug(vaug_ref, v_ref, DIFF_HEADS)

    s_len = k_ref.shape[1]
    lo = _low_lanes()
    row = qi * tq + lax.broadcasted_iota(jnp.int32, (tq, s_len), 0)
    col = lax.broadcasted_iota(jnp.int32, (tq, s_len), 1)
    dist = jnp.abs(row - col).astype(F32)
    lv = lvec_ref[...]
    lam = (jnp.exp(jnp.sum(lv[0:1] * lv[1:2], axis=-1, keepdims=True))
           - jnp.exp(jnp.sum(lv[2:3] * lv[3:4], axis=-1, keepdims=True)) + lam_init)

    def scores(h):
        q = q_ref[0, :, h * LANES:(h + 1) * LANES]
        k = k_ref[0, :, h * LANES:(h + 1) * LANES]
        zero = jnp.zeros_like(q)
        bias = (-LOG2E * 2.0 ** (-8.0 * (h + 1) / DIFF_HEADS)) * dist
        t_ref[h % 2, 0] = lax.dot_general(jnp.where(lo, q, zero), k, NT_DIMS, preferred_element_type=F32) + bias
        t_ref[h % 2, 1] = lax.dot_general(jnp.where(lo, zero, q), k, NT_DIMS, preferred_element_type=F32) + bias

    scores(0)
    for h in range(DIFF_HEADS):
        if h + 1 < DIFF_HEADS:
            scores(h + 1)
        vaug = vaug_ref[:, 2 * h * LANES:(2 * h + 2) * LANES]
        o1 = _dot(_exp_weights(t_ref[h % 2, 0]), vaug)
        o2 = _dot(_exp_weights(t_ref[h % 2, 1]), vaug)
        o = o1[:, :LANES] / o1[:, LANES:] - lam * (o2[:, :LANES] / o2[:, LANES:])
        o_ref[0, :, h * LANES:(h + 1) * LANES] = (_rms(o, g_ref[...]) * (1.0 - lam_init)).astype(o_ref.dtype)


def _diff_attention(proj, lvec, subln_g, *, lam_init, tq=256):
    b, s, _ = proj.shape
    w = DIFF_HEADS * LANES
    return pl.pallas_call(
        functools.partial(_diff_attn_body, tq=tq, lam_init=lam_init),
        grid=(b, s // tq),
        in_specs=[pl.BlockSpec((4, HEAD_DIM), lambda bi, qi: (0, 0)),
                  pl.BlockSpec((1, tq, w), lambda bi, qi: (bi, qi, 0)),
                  pl.BlockSpec((1, s, w), lambda bi, qi: (bi, 0, 1)),
                  pl.BlockSpec((1, s, w), lambda bi, qi: (bi, 0, 2)),
                  pl.BlockSpec((1, LANES), lambda bi, qi: (0, 0))],
        out_specs=pl.BlockSpec((1, tq, w), lambda bi, qi: (bi, qi, 0)),
        out_shape=jax.ShapeDtypeStruct((b, s, w), BF16),
        scratch_shapes=[pltpu.VMEM((s, 2 * w), BF16), pltpu.VMEM((2, 2, tq, s), F32)],
        compiler_params=_cparams("parallel", "arbitrary"),
        name="diff_attn",
    )(lvec, proj, proj, proj, subln_g.reshape(1, LANES))


def _na_bias_table(rpb, kh):
    qc = np.arange(GRID_W)[:, None]
    kc = np.arange(GRID_W)[None, :]
    wstart = np.clip(qc - NA_WIN_W // 2, 0, GRID_W - NA_WIN_W)
    valid = (kc >= wstart) & (kc < wstart + NA_WIN_W)
    dc = kc - qc + NA_WIN_W - 1
    onehot = ((dc[None] == np.arange(2 * NA_WIN_W - 1)[:, None, None]) & valid[None]).astype(np.float32)
    toep = jnp.einsum("hrc,cqk->hrqk", rpb.astype(F32) * LOG2E, onehot, precision=lax.Precision.HIGHEST)
    toep = jnp.where(valid[None, None], toep, NEG_INF)
    per_d = [toep[:, NA_WIN_H - 1 - d:NA_WIN_H - 1 - d + kh].transpose(0, 2, 1, 3) for d in range(kh)]
    return jnp.stack(per_d, axis=1).reshape(rpb.shape[0], kh, GRID_W, kh * GRID_W)


def _na_body(q_ref, k_ref, v_ref, bias_ref, o_ref, vaug_ref, *, rows_n, kh):
    lo = _low_lanes()
    _fill_ones_aug(vaug_ref, v_ref, 1)
    for r in range(rows_n):
        rs = min(max(r - kh // 2, 0), rows_n - kh)
        q = q_ref[0, r * GRID_W:(r + 1) * GRID_W, :]
        kwin = k_ref[0, rs * GRID_W:(rs + kh) * GRID_W, :]
        vwin = vaug_ref[rs * GRID_W:(rs + kh) * GRID_W, :]
        zero = jnp.zeros_like(q)
        q2 = jnp.concatenate([jnp.where(lo, q, zero), jnp.where(lo, zero, q)], axis=0)
        bias = bias_ref[:, r - rs].reshape(2 * GRID_W, kh * GRID_W)
        t = lax.dot_general(q2, kwin, NT_DIMS, preferred_element_type=F32) + bias
        o2 = _dot(_exp_weights(t), vwin)
        o2 = o2[:, :LANES] / o2[:, LANES:]
        o_ref[0, r * GRID_W:(r + 1) * GRID_W, :] = jnp.where(lo, o2[:GRID_W], o2[GRID_W:]).astype(o_ref.dtype)


def _na_attention(proj, rpb, *, col0):
    b, s, _ = proj.shape
    rows_n = s // GRID_W
    kh = min(NA_WIN_H, rows_n)
    pairs = NA_HEADS // 2
    bias = _na_bias_table(rpb, kh)
    return pl.pallas_call(
        functools.partial(_na_body, rows_n=rows_n, kh=kh),
        grid=(b, pairs),
        in_specs=[pl.BlockSpec((1, s, LANES), lambda bi, p: (bi, 0, col0 + p)),
                  pl.BlockSpec((1, s, LANES), lambda bi, p: (bi, 0, col0 + pairs + p)),
                  pl.BlockSpec((1, s, LANES), lambda bi, p: (bi, 0, col0 + 2 * pairs + p)),
                  pl.BlockSpec((2, kh, GRID_W, kh * GRID_W), lambda bi, p: (p, 0, 0, 0))],
        out_specs=pl.BlockSpec((1, s, LANES), lambda bi, p: (bi, 0, p)),
        out_shape=jax.ShapeDtypeStruct((b, s, pairs * LANES), BF16),
        scratch_shapes=[pltpu.VMEM((s, 2 * LANES), BF16)],
        compiler_params=_cparams("parallel", "parallel"),
        name="na_attn",
    )(proj, proj, proj, bias)


HALO = 16


def _mix_ffn_body(*refs, n_act, tiles_per_seq, final_norm):
    acts = refs[:3 * n_act]
    wos = refs[3 * n_act:4 * n_act]
    x_ref, xp_ref, xn_ref, g_ref, wg_ref, wv_ref, cw_ref, cb_ref, wd_ref, fg_ref, o_ref = refs[4 * n_act:]
    pos = pl.program_id(0) % tiles_per_seq
    tm = x_ref.shape[0]
    xe = jnp.concatenate([xp_ref[...], x_ref[...], xn_ref[...]], axis=0)
    for k in range(n_act):
        a_ref, ap_ref, an_ref = acts[3 * k:3 * k + 3]
        xe = xe + _dot(jnp.concatenate([ap_ref[...], a_ref[...], an_ref[...]], axis=0), wos[k][...])
    row = lax.broadcasted_iota(jnp.int32, (tm + 2 * HALO, 1), 0)
    first_kept = jnp.where(pos == 0, HALO, 0)
    end_kept = jnp.where(pos == tiles_per_seq - 1, HALO + tm, tm + 2 * HALO)
    xe = jnp.where((row >= first_kept) & (row < end_kept), xe, 0.0)
    y = _rms(xe, g_ref[...])
    a_ext = _dot(y.astype(BF16), wg_ref[...])
    cw = cw_ref[...]
    a = (cw[0:1] * a_ext[HALO - 1:HALO - 1 + tm] + cw[1:2] * a_ext[HALO:HALO + tm]
         + cw[2:3] * a_ext[HALO + 1:HALO + 1 + tm] + cb_ref[...])
    gate = 0.5 * a * (1.0 + lax.erf(a * SQRT_HALF))
    val = _dot(y[HALO:HALO + tm].astype(BF16), wv_ref[...])
    out = xe[HALO:HALO + tm] + _dot((gate * val).astype(BF16), wd_ref[...])
    o_ref[...] = _rms(out, fg_ref[...]) if final_norm else out


def _mix_ffn(acts, w_out, x2, s, g, w_gate, w_val, conv_w, conv_b, w_down, final_g, *, final_norm, tm=512):
    t, d = x2.shape
    dff = w_gate.shape[1]
    wa = acts[0].shape[1]
    assert all(a.shape[1] == wa for a in acts) and wa * len(acts) == w_out.shape[0]
    hb = tm // HALO
    last = t // HALO - 1

    def tiles(width):
        return [pl.BlockSpec((tm, width), lambda i: (i, 0)),
                pl.BlockSpec((HALO, width), lambda i: (jnp.maximum(i * hb - 1, 0), 0)),
                pl.BlockSpec((HALO, width), lambda i: (jnp.minimum((i + 1) * hb, last), 0))]

    def const(shape, row_block=0):
        return pl.BlockSpec(shape, lambda i: (row_block, 0), pipeline_mode=pl.Buffered(1))

    consts = (g.reshape(1, d), w_gate, w_val, conv_w, conv_b.reshape(1, dff), w_down, final_g.reshape(1, d))
    act_args = [a for act in acts for a in (act, act, act)]
    act_specs = [sp for _ in acts for sp in tiles(wa)]
    return pl.pallas_call(
        functools.partial(_mix_ffn_body, n_act=len(acts), tiles_per_seq=s // tm, final_norm=final_norm),
        grid=(t // tm,),
        in_specs=(act_specs + [const((wa, d), k) for k in range(len(acts))] + tiles(d)
                  + [const(c.shape) for c in consts]),
        out_specs=pl.BlockSpec((tm, d), lambda i: (i, 0)),
        out_shape=jax.ShapeDtypeStruct((t, d), F32),
        compiler_params=_cparams("parallel"),
        name="mix_ffn",
    )(*act_args, *([w_out] * len(acts)), x2, x2, x2, *consts)


def _mla_prep_body(x_ref, g_ref, wdq_ref, qg_ref, wuq_ref, wuqs_ref, wdkv_ref, kvg_ref, wk_ref, wv_ref,
                   ct_ref, st_ref, q_out, k_out, v_out):
    hn = _rms(x_ref[...], g_ref[...]).astype(BF16)
    cq = _rms(_dot(hn, wdq_ref[...]), qg_ref[...]).astype(BF16)
    kva = _dot(hn, wdkv_ref[...])
    ckv = _rms(kva[:, :MLA_KV_RANK], kvg_ref[...]).astype(BF16)
    ct = ct_ref[...]
    st = st_ref[...]
    kr = kva[:, MLA_KV_RANK:MLA_KV_RANK + LANES]
    kr_sw = kva[:, MLA_KV_RANK + LANES:]
    roped = kr * ct[:, :LANES] + kr_sw * st[:, :LANES]
    roped2 = jnp.concatenate([roped, roped], axis=1)
    w2 = 2 * LANES
    for j in range(0, q_out.shape[1], w2):
        qh = _dot(cq, wuq_ref[:, j:j + w2])
        qs = _dot(cq, wuqs_ref[:, j:j + w2])
        q_out[:, j:j + w2] = ((qh * ct + qs * st) * MLA_SOFTMAX_SCALE).astype(q_out.dtype)
        k_out[:, j:j + w2] = (_dot(ckv, wk_ref[:, j:j + w2]) + roped2).astype(k_out.dtype)
    v_out[...] = _dot(ckv, wv_ref[...]).astype(v_out.dtype)


def _mla_prep(x2, s, g, wdq, qg, wuq, wuqs, wdkv, kvg, wk, wv, ct, st, *, tm=512):
    t, d = x2.shape
    nblk = s // tm
    full = lambda a: pl.BlockSpec(a.shape, lambda i: (0, 0))
    hw = MLA_HEADS * LANES
    return pl.pallas_call(
        _mla_prep_body,
        grid=(t // tm,),
        in_specs=[pl.BlockSpec((tm, d), lambda i: (i, 0)), full(g), full(wdq), full(qg), full(wuq), full(wuqs),
                  full(wdkv), full(kvg), full(wk), full(wv),
                  pl.BlockSpec((tm, 2 * LANES), lambda i: (i % nblk, 0)),
                  pl.BlockSpec((tm, 2 * LANES), lambda i: (i % nblk, 0))],
        out_specs=[pl.BlockSpec((tm, hw), lambda i: (i, 0)),
                   pl.BlockSpec((tm, hw), lambda i: (i, 0)),
                   pl.BlockSpec((tm, MLA_HEADS * MLA_V), lambda i: (i, 0))],
        out_shape=[jax.ShapeDtypeStruct((t, hw), BF16),
                   jax.ShapeDtypeStruct((t, hw), BF16),
                   jax.ShapeDtypeStruct((t, MLA_HEADS * MLA_V), BF16)],
        compiler_params=_cparams("parallel"),
        name="mla_prep",
    )(x2, g, wdq, qg, wuq, wuqs, wdkv, kvg, wk, wv, ct, st)


def _mla_attn_body(q_ref, k_ref, v_ref, o_ref, vaug_ref, t_ref, *, heads):
    lo = _low_lanes()

    @pl.when(pl.program_id(2) == 0)
    def _():
        _fill_ones_aug(vaug_ref, v_ref, heads // 2)

    def scores(hh):
        q = q_ref[0, :, hh * LANES:(hh + 1) * LANES]
        k = k_ref[0, :, hh * LANES:(hh + 1) * LANES]
        t_ref[hh % 2] = lax.dot_general(q, k, NT_DIMS, preferred_element_type=F32)

    scores(0)
    outs = []
    for hh in range(heads):
        if hh + 1 < heads:
            scores(hh + 1)
        p = hh // 2
        o = _dot(_exp_weights(t_ref[hh % 2]), vaug_ref[:, 2 * p * LANES:(2 * p + 2) * LANES])
        outs.append(o[:, :LANES] / o[:, LANES:])
        if hh % 2 == 1:
            o_ref[0, :, p * LANES:(p + 1) * LANES] = jnp.where(lo, outs[0], outs[1]).astype(o_ref.dtype)
            outs = []


def _mla_attention(q, k, v, *, tq=256, heads=8):
    b, s, _ = q.shape
    groups = MLA_HEADS // heads
    return pl.pallas_call(
        functools.partial(_mla_attn_body, heads=heads),
        grid=(b, groups, s // tq),
        in_specs=[pl.BlockSpec((1, tq, heads * LANES), lambda bi, g, qi: (bi, qi, g)),
                  pl.BlockSpec((1, s, heads * LANES), lambda bi, g, qi: (bi, 0, g)),
                  pl.BlockSpec((1, s, heads * MLA_V), lambda bi, g, qi: (bi, 0, g))],
        out_specs=pl.BlockSpec((1, tq, heads * MLA_V), lambda bi, g, qi: (bi, qi, g)),
        out_shape=jax.ShapeDtypeStruct((b, s, MLA_HEADS * MLA_V), BF16),
        scratch_shapes=[pltpu.VMEM((s, heads * LANES), BF16), pltpu.VMEM((2, tq, s), F32)],
        compiler_params=_cparams("parallel", "parallel", "arbitrary"),
        name="mla_attn",
    )(q, k, v)


def _mla_weights(w_uq, w_dkv, w_ukv):
    f = MLA_NOPE + MLA_ROPE
    half = MLA_ROPE // 2
    r = w_uq.shape[0]
    uq = w_uq.reshape(r, MLA_HEADS, f)
    zq = jnp.zeros((r, MLA_HEADS, LANES - f), w_uq.dtype)
    wuq = jnp.concatenate([uq, zq], axis=-1).reshape(r, MLA_HEADS * LANES)
    wuqs = jnp.concatenate([jnp.zeros((r, MLA_HEADS, MLA_NOPE), w_uq.dtype),
                            uq[..., MLA_NOPE + half:], uq[..., MLA_NOPE:MLA_NOPE + half], zq],
                           axis=-1).reshape(r, MLA_HEADS * LANES)
    d = w_dkv.shape[0]
    kr = w_dkv[:, MLA_KV_RANK:]
    z64 = jnp.zeros((d, MLA_NOPE), w_dkv.dtype)
    z32 = jnp.zeros((d, LANES - f), w_dkv.dtype)
    wdkv = jnp.concatenate([w_dkv[:, :MLA_KV_RANK], z64, kr, z32,
                            z64, kr[:, half:], kr[:, :half], z32], axis=-1)
    c = w_ukv.shape[0]
    ukv = w_ukv.reshape(c, MLA_HEADS, MLA_NOPE + MLA_V)
    wk = jnp.concatenate([ukv[..., :MLA_NOPE], jnp.zeros((c, MLA_HEADS, LANES - MLA_NOPE), w_ukv.dtype)],
                         axis=-1).reshape(c, MLA_HEADS * LANES)
    wv = ukv[..., MLA_NOPE:].reshape(c, MLA_HEADS * MLA_V)
    return wuq.astype(BF16), wuqs.astype(BF16), wdkv.astype(BF16), wk.astype(BF16), wv.astype(BF16)


def _rope_lane_tables(s):
    inv = ROPE_THETA ** (-jnp.arange(0, MLA_ROPE, 2, dtype=F32) / MLA_ROPE)
    ang = jnp.arange(s, dtype=F32)[:, None] * inv[None, :]
    cos, sin = jnp.cos(ang), jnp.sin(ang)
    one = jnp.ones((s, MLA_NOPE), F32)
    z64 = jnp.zeros((s, MLA_NOPE), F32)
    z32 = jnp.zeros((s, LANES - MLA_NOPE - MLA_ROPE), F32)
    ct = jnp.concatenate([one, cos, cos, z32], axis=-1)
    st = jnp.concatenate([z64, -sin, sin, z32], axis=-1)
    return jnp.tile(ct, (1, 2)), jnp.tile(st, (1, 2))


def kernel(x, mix_norm_e, w_in_e, diff_lq1, diff_lk1, diff_lq2, diff_lk2, diff_subln_g, na_rpb, w_out_e,
           mix_norm_o, w_dq, q_norm_g, w_uq, w_dkv, kv_norm_g, w_ukv, w_o_mla, ffn_norm_g, w_ffn_gate,
           w_ffn_val, ffn_conv_w, ffn_conv_b, w_ffn_down, final_norm_g):
    b, s, d = x.shape
    t = b * s
    depth = ffn_norm_g.shape[0]
    for i in range(depth):
        j = i // 2
        x2 = x.reshape(t, d)
        if i % 2 == 0:
            lam_init = 0.8 - 0.6 * math.exp(-0.3 * i)
            wq = DIFF_HEADS * LANES
            col_scale = np.ones((1, w_in_e.shape[-1]), np.float32)
            col_scale[:, :wq] = SOFTMAX_SCALE
            col_scale[:, 3 * wq:3 * wq + NA_HEADS * HEAD_DIM] = SOFTMAX_SCALE
            proj = _norm_proj(x2, mix_norm_e[j], w_in_e[j].astype(BF16), jnp.asarray(col_scale)).reshape(b, s, -1)
            lvec = jnp.stack([diff_lq1[j], diff_lk1[j], diff_lq2[j], diff_lk2[j]]).astype(F32)
            a = _diff_attention(proj, lvec, diff_subln_g[j], lam_init=lam_init)
            nb = _na_attention(proj, na_rpb[j], col0=3 * DIFF_HEADS)
            acts, w_out = [a.reshape(t, -1), nb.reshape(t, -1)], w_out_e[j].astype(BF16)
        else:
            wuq, wuqs, wdkv, wk, wv = _mla_weights(w_uq[j], w_dkv[j], w_ukv[j])
            ct, st = _rope_lane_tables(s)
            q, k, v = _mla_prep(x2, s, mix_norm_o[j].reshape(1, d), w_dq[j].astype(BF16),
                                q_norm_g[j].reshape(1, -1), wuq, wuqs, wdkv, kv_norm_g[j].reshape(1, -1),
                                wk, wv, ct, st)
            o = _mla_attention(q.reshape(b, s, -1), k.reshape(b, s, -1), v.reshape(b, s, -1))
            acts, w_out = [o.reshape(t, -1)], w_o_mla[j].astype(BF16)
        x = _mix_ffn(acts, w_out, x2, s, ffn_norm_g[i], w_ffn_gate[i].astype(BF16), w_ffn_val[i].astype(BF16),
                     ffn_conv_w[i], ffn_conv_b[i], w_ffn_down[i].astype(BF16), final_norm_g,
                     final_norm=(i == depth - 1)).reshape(b, s, d)
    return x
```

```python
import functools
import math

import numpy as np
import jax
import jax.numpy as jnp
from jax import lax
from jax.experimental import pallas as pl
from jax.experimental.pallas import tpu as pltpu

F32 = jnp.float32
BF16 = jnp.bfloat16

EPS = 1e-6
NEG_INF = -1e30
GRID_W = 64
HEAD_DIM = 64
DIFF_HEADS = 4
NA_HEADS = 8
NA_WIN_H = 8
NA_WIN_W = 16
MLA_HEADS = 16
MLA_NOPE = 64
MLA_ROPE = 32
MLA_V = 64
MLA_KV_RANK = 256
ROPE_THETA = 10000.0

LANES = 128
VMEM_LIMIT = 56 * 1024 * 1024
NT_DIMS = (((1,), (1,)), ((), ()))
LOG2E = math.log2(math.e)
SQRT_HALF = np.float32(math.sqrt(0.5))
SOFTMAX_SCALE = HEAD_DIM ** -0.5 * LOG2E
MLA_SOFTMAX_SCALE = (MLA_NOPE + MLA_ROPE) ** -0.5 * LOG2E
BOUND_SLACK = 1.0 + 2.0 ** -6
DENOM_MIN = 2.0 ** -60
MLA_AUG_LANE = MLA_NOPE + MLA_ROPE


def _cparams(*sem):
    return pltpu.CompilerParams(dimension_semantics=sem, vmem_limit_bytes=VMEM_LIMIT)


def _rms(xf, g):
    ms = jnp.mean(xf * xf, axis=-1, keepdims=True)
    return xf * lax.rsqrt(ms + EPS) * g


def _dot(a, b):
    return jnp.dot(a, b, preferred_element_type=F32)


def _low_lanes():
    return lax.broadcasted_iota(jnp.int32, (1, LANES), 1) < HEAD_DIM


def _exp_weights(t):
    return jnp.exp2(t - jnp.max(t, axis=-1, keepdims=True)).astype(BF16)


def _fill_ones_aug(vaug_ref, v_ref, n_blocks):
    ones = jnp.ones((v_ref.shape[1], LANES), BF16)
    for p in range(n_blocks):
        vaug_ref[:, 2 * p * LANES:(2 * p + 1) * LANES] = v_ref[0, :, p * LANES:(p + 1) * LANES]
        vaug_ref[:, (2 * p + 1) * LANES:(2 * p + 2) * LANES] = ones


def _norm_proj_body(x_ref, g_ref, w_ref, cs_ref, o_ref, *, n_chunk):
    hn = _rms(x_ref[...], g_ref[...]).astype(BF16)
    for j in range(0, o_ref.shape[1], n_chunk):
        o_ref[:, j:j + n_chunk] = (_dot(hn, w_ref[:, j:j + n_chunk]) * cs_ref[:, j:j + n_chunk]).astype(o_ref.dtype)


def _norm_proj(x2, g, w, col_scale, *, tm=512, n_chunk=768):
    t, d = x2.shape
    n = w.shape[1]
    return pl.pallas_call(
        functools.partial(_norm_proj_body, n_chunk=n_chunk),
        grid=(t // tm,),
        in_specs=[pl.BlockSpec((tm, d), lambda i: (i, 0)),
                  pl.BlockSpec((1, d), lambda i: (0, 0)),
                  pl.BlockSpec((d, n), lambda i: (0, 0)),
                  pl.BlockSpec((1, n), lambda i: (0, 0))],
        out_specs=pl.BlockSpec((tm, n), lambda i: (i, 0)),
        out_shape=jax.ShapeDtypeStruct((t, n), BF16),
        compiler_params=_cparams("parallel"),
        name="norm_proj",
    )(x2, g.reshape(1, d), w, col_scale)


def _diff_attn_body(lvec_ref, q_ref, k_ref, v_ref, g_ref, o_ref, vaug_ref, kmax_ref, t_ref, *, tq, lam_init):
    qi = pl.program_id(1)
    lo = _low_lanes()

    @pl.when(qi == 0)
    def _():
        _fill_ones_aug(vaug_ref, v_ref, DIFF_HEADS)
        for h in range(DIFF_HEADS):
            kf = k_ref[0, :, h * LANES:(h + 1) * LANES].astype(F32)
            k2 = kf * kf
            n1 = jnp.max(jnp.sum(jnp.where(lo, k2, 0.0), axis=-1, keepdims=True), axis=0, keepdims=True)
            n2 = jnp.max(jnp.sum(jnp.where(lo, 0.0, k2), axis=-1, keepdims=True), axis=0, keepdims=True)
            kmax_ref[2 * h] = jnp.broadcast_to(jnp.sqrt(n1), kmax_ref.shape[1:])
            kmax_ref[2 * h + 1] = jnp.broadcast_to(jnp.sqrt(n2), kmax_ref.shape[1:])

    s_len = k_ref.shape[1]
    row = qi * tq + lax.broadcasted_iota(jnp.int32, (tq, s_len), 0)
    col = lax.broadcasted_iota(jnp.int32, (tq, s_len), 1)
    dist = jnp.abs(row - col).astype(F32)
    lv = lvec_ref[...]
    lam = (jnp.exp(jnp.sum(lv[0:1] * lv[1:2], axis=-1, keepdims=True))
           - jnp.exp(jnp.sum(lv[2:3] * lv[3:4], axis=-1, keepdims=True)) + lam_init)

    def maps(h):
        q = q_ref[0, :, h * LANES:(h + 1) * LANES]
        k = k_ref[0, :, h * LANES:(h + 1) * LANES]
        zero = jnp.zeros_like(q)
        bias = (-LOG2E * 2.0 ** (-8.0 * (h + 1) / DIFF_HEADS)) * dist
        s1 = lax.dot_general(jnp.where(lo, q, zero), k, NT_DIMS, preferred_element_type=F32)
        s2 = lax.dot_general(jnp.where(lo, zero, q), k, NT_DIMS, preferred_element_type=F32)
        return q, s1, s2, bias

    def finish(h, o1, o2):
        o = o1[:, :LANES] / o1[:, LANES:] - lam * (o2[:, :LANES] / o2[:, LANES:])
        o_ref[0, :, h * LANES:(h + 1) * LANES] = (_rms(o, g_ref[...]) * (1.0 - lam_init)).astype(o_ref.dtype)

    lmin = None
    for h in range(DIFF_HEADS):
        q, s1, s2, bias = maps(h)
        q2 = jnp.square(q.astype(F32))
        b1 = (jnp.sqrt(jnp.sum(jnp.where(lo, q2, 0.0), axis=-1, keepdims=True))
              * (kmax_ref[2 * h][0:1, 0:1] * BOUND_SLACK))
        b2 = (jnp.sqrt(jnp.sum(jnp.where(lo, 0.0, q2), axis=-1, keepdims=True))
              * (kmax_ref[2 * h + 1][0:1, 0:1] * BOUND_SLACK))
        vaug = vaug_ref[:, 2 * h * LANES:(2 * h + 2) * LANES]
        o1 = _dot(jnp.exp2(s1 + (bias - b1)).astype(BF16), vaug)
        o2 = _dot(jnp.exp2(s2 + (bias - b2)).astype(BF16), vaug)
        finish(h, o1, o2)
        l = jnp.minimum(o1[:, LANES:], o2[:, LANES:])
        lmin = l if lmin is None else jnp.minimum(lmin, l)

    @pl.when(jnp.logical_not(jnp.min(lmin) >= DENOM_MIN))
    def _():
        for h in range(DIFF_HEADS):
            _, s1, s2, bias = maps(h)
            t_ref[0] = s1 + bias
            t_ref[1] = s2 + bias
            vaug = vaug_ref[:, 2 * h * LANES:(2 * h + 2) * LANES]
            finish(h, _dot(_exp_weights(t_ref[0]), vaug), _dot(_exp_weights(t_ref[1]), vaug))


def _diff_attention(proj, lvec, subln_g, *, lam_init, tq=512):
    b, s, _ = proj.shape
    w = DIFF_HEADS * LANES
    return pl.pallas_call(
        functools.partial(_diff_attn_body, tq=tq, lam_init=lam_init),
        grid=(b, s // tq),
        in_specs=[pl.BlockSpec((4, HEAD_DIM), lambda bi, qi: (0, 0)),
                  pl.BlockSpec((1, tq, w), lambda bi, qi: (bi, qi, 0)),
                  pl.BlockSpec((1, s, w), lambda bi, qi: (bi, 0, 1)),
                  pl.BlockSpec((1, s, w), lambda bi, qi: (bi, 0, 2)),
                  pl.BlockSpec((1, LANES), lambda bi, qi: (0, 0))],
        out_specs=pl.BlockSpec((1, tq, w), lambda bi, qi: (bi, qi, 0)),
        out_shape=jax.ShapeDtypeStruct((b, s, w), BF16),
        scratch_shapes=[pltpu.VMEM((s, 2 * w), BF16), pltpu.VMEM((2 * DIFF_HEADS, 8, LANES), F32),
                        pltpu.VMEM((2, tq, s), F32)],
        compiler_params=_cparams("parallel", "arbitrary"),
        name="diff_attn",
    )(lvec, proj, proj, proj, subln_g.reshape(1, LANES))


def _na_bias_table(rpb, kh):
    qc = np.arange(GRID_W)[:, None]
    kc = np.arange(GRID_W)[None, :]
    wstart = np.clip(qc - NA_WIN_W // 2, 0, GRID_W - NA_WIN_W)
    valid = (kc >= wstart) & (kc < wstart + NA_WIN_W)
    dc = kc - qc + NA_WIN_W - 1
    onehot = ((dc[None] == np.arange(2 * NA_WIN_W - 1)[:, None, None]) & valid[None]).astype(np.float32)
    toep = jnp.einsum("hrc,cqk->hrqk", rpb.astype(F32) * LOG2E, onehot, precision=lax.Precision.HIGHEST)
    toep = jnp.where(valid[None, None], toep, NEG_INF)
    per_d = [toep[:, NA_WIN_H - 1 - d:NA_WIN_H - 1 - d + kh].transpose(0, 2, 1, 3) for d in range(kh)]
    return jnp.stack(per_d, axis=1).reshape(rpb.shape[0], kh, GRID_W, kh * GRID_W)


def _na_body(q_ref, k_ref, v_ref, bias_ref, o_ref, vaug_ref, *, rows_n, kh):
    lo = _low_lanes()
    _fill_ones_aug(vaug_ref, v_ref, 1)
    for r in range(rows_n):
        rs = min(max(r - kh // 2, 0), rows_n - kh)
        q = q_ref[0, r * GRID_W:(r + 1) * GRID_W, :]
        kwin = k_ref[0, rs * GRID_W:(rs + kh) * GRID_W, :]
        vwin = vaug_ref[rs * GRID_W:(rs + kh) * GRID_W, :]
        zero = jnp.zeros_like(q)
        q2 = jnp.concatenate([jnp.where(lo, q, zero), jnp.where(lo, zero, q)], axis=0)
        bias = bias_ref[:, r - rs].reshape(2 * GRID_W, kh * GRID_W)
        t = lax.dot_general(q2, kwin, NT_DIMS, preferred_element_type=F32) + bias
        o2 = _dot(_exp_weights(t), vwin)
        o2 = o2[:, :LANES] / o2[:, LANES:]
        o_ref[0, r * GRID_W:(r + 1) * GRID_W, :] = jnp.where(lo, o2[:GRID_W], o2[GRID_W:]).astype(o_ref.dtype)


def _na_attention(proj, rpb, *, col0):
    b, s, _ = proj.shape
    rows_n = s // GRID_W
    kh = min(NA_WIN_H, rows_n)
    pairs = NA_HEADS // 2
    bias = _na_bias_table(rpb, kh)
    return pl.pallas_call(
        functools.partial(_na_body, rows_n=rows_n, kh=kh),
        grid=(b, pairs),
        in_specs=[pl.BlockSpec((1, s, LANES), lambda bi, p: (bi, 0, col0 + p)),
                  pl.BlockSpec((1, s, LANES), lambda bi, p: (bi, 0, col0 + pairs + p)),
                  pl.BlockSpec((1, s, LANES), lambda bi, p: (bi, 0, col0 + 2 * pairs + p)),
                  pl.BlockSpec((2, kh, GRID_W, kh * GRID_W), lambda bi, p: (p, 0, 0, 0))],
        out_specs=pl.BlockSpec((1, s, LANES), lambda bi, p: (bi, 0, p)),
        out_shape=jax.ShapeDtypeStruct((b, s, pairs * LANES), BF16),
        scratch_shapes=[pltpu.VMEM((s, 2 * LANES), BF16)],
        compiler_params=_cparams("parallel", "parallel"),
        name="na_attn",
    )(proj, proj, proj, bias)


HALO = 16


def _mix_ffn_body(*refs, n_act, tiles_per_seq, final_norm):
    acts = refs[:3 * n_act]
    wos = refs[3 * n_act:4 * n_act]
    x_ref, xp_ref, xn_ref, g_ref, wg_ref, wv_ref, cw_ref, cb_ref, wd_ref, fg_ref, o_ref = refs[4 * n_act:]
    pos = pl.program_id(0) % tiles_per_seq
    tm = x_ref.shape[0]
    xe = jnp.concatenate([xp_ref[...], x_ref[...], xn_ref[...]], axis=0)
    for k in range(n_act):
        a_ref, ap_ref, an_ref = acts[3 * k:3 * k + 3]
        xe = xe + _dot(jnp.concatenate([ap_ref[...], a_ref[...], an_ref[...]], axis=0), wos[k][...])
    row = lax.broadcasted_iota(jnp.int32, (tm + 2 * HALO, 1), 0)
    first_kept = jnp.where(pos == 0, HALO, 0)
    end_kept = jnp.where(pos == tiles_per_seq - 1, HALO + tm, tm + 2 * HALO)
    xe = jnp.where((row >= first_kept) & (row < end_kept), xe, 0.0)
    y = _rms(xe, g_ref[...])
    a_ext = _dot(y.astype(BF16), wg_ref[...])
    cw = cw_ref[...]
    a = (cw[0:1] * a_ext[HALO - 1:HALO - 1 + tm] + cw[1:2] * a_ext[HALO:HALO + tm]
         + cw[2:3] * a_ext[HALO + 1:HALO + 1 + tm] + cb_ref[...])
    gate = 0.5 * a * (1.0 + lax.erf(a * SQRT_HALF))
    val = _dot(y[HALO:HALO + tm].astype(BF16), wv_ref[...])
    out = xe[HALO:HALO + tm] + _dot((gate * val).astype(BF16), wd_ref[...])
    o_ref[...] = _rms(out, fg_ref[...]) if final_norm else out


def _mix_ffn(acts, w_out, x2, s, g, w_gate, w_val, conv_w, conv_b, w_down, final_g, *, final_norm, tm=512):
    t, d = x2.shape
    dff = w_gate.shape[1]
    wa = acts[0].shape[1]
    assert all(a.shape[1] == wa for a in acts) and wa * len(acts) == w_out.shape[0]
    hb = tm // HALO
    last = t // HALO - 1

    def tiles(width):
        return [pl.BlockSpec((tm, width), lambda i: (i, 0)),
                pl.BlockSpec((HALO, width), lambda i: (jnp.maximum(i * hb - 1, 0), 0)),
                pl.BlockSpec((HALO, width), lambda i: (jnp.minimum((i + 1) * hb, last), 0))]

    def const(shape, row_block=0):
        return pl.BlockSpec(shape, lambda i: (row_block, 0), pipeline_mode=pl.Buffered(1))

    consts = (g.reshape(1, d), w_gate, w_val, conv_w, conv_b.reshape(1, dff), w_down, final_g.reshape(1, d))
    act_args = [a for act in acts for a in (act, act, act)]
    act_specs = [sp for _ in acts for sp in tiles(wa)]
    return pl.pallas_call(
        functools.partial(_mix_ffn_body, n_act=len(acts), tiles_per_seq=s // tm, final_norm=final_norm),
        grid=(t // tm,),
        in_specs=(act_specs + [const((wa, d), k) for k in range(len(acts))] + tiles(d)
                  + [const(c.shape) for c in consts]),
        out_specs=pl.BlockSpec((tm, d), lambda i: (i, 0)),
        out_shape=jax.ShapeDtypeStruct((t, d), F32),
        compiler_params=_cparams("parallel"),
        name="mix_ffn",
    )(*act_args, *([w_out] * len(acts)), x2, x2, x2, *consts)


def _mla_prep_body(x_ref, g_ref, wdq_ref, qg_ref, wuq_ref, wuqs_ref, wdkv_ref, kvg_ref, wk_ref, wv_ref,
                   ct_ref, st_ref, q_out, k_out, v_out):
    hn = _rms(x_ref[...], g_ref[...]).astype(BF16)
    cq = _rms(_dot(hn, wdq_ref[...]), qg_ref[...]).astype(BF16)
    kva = _dot(hn, wdkv_ref[...])
    ckv = _rms(kva[:, :MLA_KV_RANK], kvg_ref[...]).astype(BF16)
    ct = ct_ref[...]
    st = st_ref[...]
    kr = kva[:, MLA_KV_RANK:MLA_KV_RANK + LANES]
    kr_sw = kva[:, MLA_KV_RANK + LANES:]
    roped = kr * ct[:, :LANES] + kr_sw * st[:, :LANES]
    shift_lane = lax.broadcasted_iota(jnp.int32, (1, LANES), 1) == MLA_AUG_LANE
    roped = jnp.where(shift_lane, 1.0, roped)
    roped2 = jnp.concatenate([roped, roped], axis=1)
    w2 = 2 * LANES
    for j in range(0, q_out.shape[1], w2):
        qh = _dot(cq, wuq_ref[:, j:j + w2])
        qs = _dot(cq, wuqs_ref[:, j:j + w2])
        q_out[:, j:j + w2] = ((qh * ct + qs * st) * MLA_SOFTMAX_SCALE).astype(q_out.dtype)
        k_out[:, j:j + w2] = (_dot(ckv, wk_ref[:, j:j + w2]) + roped2).astype(k_out.dtype)
    v_out[...] = _dot(ckv, wv_ref[...]).astype(v_out.dtype)


def _mla_prep(x2, s, g, wdq, qg, wuq, wuqs, wdkv, kvg, wk, wv, ct, st, *, tm=512):
    t, d = x2.shape
    nblk = s // tm
    full = lambda a: pl.BlockSpec(a.shape, lambda i: (0, 0))
    hw = MLA_HEADS * LANES
    return pl.pallas_call(
        _mla_prep_body,
        grid=(t // tm,),
        in_specs=[pl.BlockSpec((tm, d), lambda i: (i, 0)), full(g), full(wdq), full(qg), full(wuq), full(wuqs),
                  full(wdkv), full(kvg), full(wk), full(wv),
                  pl.BlockSpec((tm, 2 * LANES), lambda i: (i % nblk, 0)),
                  pl.BlockSpec((tm, 2 * LANES), lambda i: (i % nblk, 0))],
        out_specs=[pl.BlockSpec((tm, hw), lambda i: (i, 0)),
                   pl.BlockSpec((tm, hw), lambda i: (i, 0)),
                   pl.BlockSpec((tm, MLA_HEADS * MLA_V), lambda i: (i, 0))],
        out_shape=[jax.ShapeDtypeStruct((t, hw), BF16),
                   jax.ShapeDtypeStruct((t, hw), BF16),
                   jax.ShapeDtypeStruct((t, MLA_HEADS * MLA_V), BF16)],
        compiler_params=_cparams("parallel"),
        name="mla_prep",
    )(x2, g, wdq, qg, wuq, wuqs, wdkv, kvg, wk, wv, ct, st)


def _mla_attn_body(q_ref, k_ref, v_ref, o_ref, vaug_ref, kmax_ref, t_ref, *, heads):
    lo = _low_lanes()
    shift_lane = lax.broadcasted_iota(jnp.int32, (1, LANES), 1) == MLA_AUG_LANE

    @pl.when(pl.program_id(2) == 0)
    def _():
        _fill_ones_aug(vaug_ref, v_ref, heads // 2)
        for hh in range(heads):
            kf = k_ref[0, :, hh * LANES:(hh + 1) * LANES].astype(F32)
            n = jnp.max(jnp.sum(jnp.where(shift_lane, 0.0, kf * kf), axis=-1, keepdims=True), axis=0, keepdims=True)
            kmax_ref[hh] = jnp.broadcast_to(jnp.sqrt(n), kmax_ref.shape[1:])

    def finish(hh, o, outs):
        p = hh // 2
        outs.append(o[:, :LANES] / o[:, LANES:])
        if hh % 2 == 1:
            o_ref[0, :, p * LANES:(p + 1) * LANES] = jnp.where(lo, outs[0], outs[1]).astype(o_ref.dtype)
            outs.clear()

    lmin = None
    outs = []
    for hh in range(heads):
        q = q_ref[0, :, hh * LANES:(hh + 1) * LANES]
        k = k_ref[0, :, hh * LANES:(hh + 1) * LANES]
        qn = jnp.sqrt(jnp.sum(jnp.square(q.astype(F32)), axis=-1, keepdims=True))
        shift = (qn * (kmax_ref[hh][0:1, 0:1] * -BOUND_SLACK)).astype(BF16)
        e = jnp.exp2(lax.dot_general(jnp.where(shift_lane, shift, q), k, NT_DIMS, preferred_element_type=F32))
        o = _dot(e.astype(BF16), vaug_ref[:, 2 * (hh // 2) * LANES:(2 * (hh // 2) + 2) * LANES])
        finish(hh, o, outs)
        lmin = o[:, LANES:] if lmin is None else jnp.minimum(lmin, o[:, LANES:])

    @pl.when(jnp.logical_not(jnp.min(lmin) >= DENOM_MIN))
    def _():
        outs = []
        for hh in range(heads):
            q = q_ref[0, :, hh * LANES:(hh + 1) * LANES]
            k = k_ref[0, :, hh * LANES:(hh + 1) * LANES]
            t_ref[...] = lax.dot_general(q, k, NT_DIMS, preferred_element_type=F32)
            finish(hh, _dot(_exp_weights(t_ref[...]), vaug_ref[:, 2 * (hh // 2) * LANES:(2 * (hh // 2) + 2) * LANES]),
                   outs)


def _mla_attention(q, k, v, *, tq=512, heads=8):
    b, s, _ = q.shape
    groups = MLA_HEADS // heads
    return pl.pallas_call(
        functools.partial(_mla_attn_body, heads=heads),
        grid=(b, groups, s // tq),
        in_specs=[pl.BlockSpec((1, tq, heads * LANES), lambda bi, g, qi: (bi, qi, g)),
                  pl.BlockSpec((1, s, heads * LANES), lambda bi, g, qi: (bi, 0, g)),
                  pl.BlockSpec((1, s, heads * MLA_V), lambda bi, g, qi: (bi, 0, g))],
        out_specs=pl.BlockSpec((1, tq, heads * MLA_V), lambda bi, g, qi: (bi, qi, g)),
        out_shape=jax.ShapeDtypeStruct((b, s, MLA_HEADS * MLA_V), BF16),
        scratch_shapes=[pltpu.VMEM((s, heads * LANES), BF16), pltpu.VMEM((heads, 8, LANES), F32),
                        pltpu.VMEM((tq, s), F32)],
        compiler_params=_cparams("parallel", "parallel", "arbitrary"),
        name="mla_attn",
    )(q, k, v)


def _mla_weights(w_uq, w_dkv, w_ukv):
    f = MLA_NOPE + MLA_ROPE
    half = MLA_ROPE // 2
    r = w_uq.shape[0]
    uq = w_uq.reshape(r, MLA_HEADS, f)
    zq = jnp.zeros((r, MLA_HEADS, LANES - f), w_uq.dtype)
    wuq = jnp.concatenate([uq, zq], axis=-1).reshape(r, MLA_HEADS * LANES)
    wuqs = jnp.concatenate([jnp.zeros((r, MLA_HEADS, MLA_NOPE), w_uq.dtype),
                            uq[..., MLA_NOPE + half:], uq[..., MLA_NOPE:MLA_NOPE + half], zq],
                           axis=-1).reshape(r, MLA_HEADS * LANES)
    d = w_dkv.shape[0]
    kr = w_dkv[:, MLA_KV_RANK:]
    z64 = jnp.zeros((d, MLA_NOPE), w_dkv.dtype)
    z32 = jnp.zeros((d, LANES - f), w_dkv.dtype)
    wdkv = jnp.concatenate([w_dkv[:, :MLA_KV_RANK], z64, kr, z32,
                            z64, kr[:, half:], kr[:, :half], z32], axis=-1)
    c = w_ukv.shape[0]
    ukv = w_ukv.reshape(c, MLA_HEADS, MLA_NOPE + MLA_V)
    wk = jnp.concatenate([ukv[..., :MLA_NOPE], jnp.zeros((c, MLA_HEADS, LANES - MLA_NOPE), w_ukv.dtype)],
                         axis=-1).reshape(c, MLA_HEADS * LANES)
    wv = ukv[..., MLA_NOPE:].reshape(c, MLA_HEADS * MLA_V)
    return wuq.astype(BF16), wuqs.astype(BF16), wdkv.astype(BF16), wk.astype(BF16), wv.astype(BF16)


def _rope_lane_tables(s):
    inv = ROPE_THETA ** (-jnp.arange(0, MLA_ROPE, 2, dtype=F32) / MLA_ROPE)
    ang = jnp.arange(s, dtype=F32)[:, None] * inv[None, :]
    cos, sin = jnp.cos(ang), jnp.sin(ang)
    one = jnp.ones((s, MLA_NOPE), F32)
    z64 = jnp.zeros((s, MLA_NOPE), F32)
    z32 = jnp.zeros((s, LANES - MLA_NOPE - MLA_ROPE), F32)
    ct = jnp.concatenate([one, cos, cos, z32], axis=-1)
    st = jnp.concatenate([z64, -sin, sin, z32], axis=-1)
    return jnp.tile(ct, (1, 2)), jnp.tile(st, (1, 2))


def kernel(x, mix_norm_e, w_in_e, diff_lq1, diff_lk1, diff_lq2, diff_lk2, diff_subln_g, na_rpb, w_out_e,
           mix_norm_o, w_dq, q_norm_g, w_uq, w_dkv, kv_norm_g, w_ukv, w_o_mla, ffn_norm_g, w_ffn_gate,
           w_ffn_val, ffn_conv_w, ffn_conv_b, w_ffn_down, final_norm_g):
    b, s, d = x.shape
    t = b * s
    depth = ffn_norm_g.shape[0]
    for i in range(depth):
        j = i // 2
        x2 = x.reshape(t, d)
        if i % 2 == 0:
            lam_init = 0.8 - 0.6 * math.exp(-0.3 * i)
            wq = DIFF_HEADS * LANES
            col_scale = np.ones((1, w_in_e.shape[-1]), np.float32)
            col_scale[:, :wq] = SOFTMAX_SCALE
            col_scale[:, 3 * wq:3 * wq + NA_HEADS * HEAD_DIM] = SOFTMAX_SCALE
            proj = _norm_proj(x2, mix_norm_e[j], w_in_e[j].astype(BF16), jnp.asarray(col_scale)).reshape(b, s, -1)
            lvec = jnp.stack([diff_lq1[j], diff_lk1[j], diff_lq2[j], diff_lk2[j]]).astype(F32)
            a = _diff_attention(proj, lvec, diff_subln_g[j], lam_init=lam_init)
            nb = _na_attention(proj, na_rpb[j], col0=3 * DIFF_HEADS)
            acts, w_out = [a.reshape(t, -1), nb.reshape(t, -1)], w_out_e[j].astype(BF16)
        else:
            wuq, wuqs, wdkv, wk, wv = _mla_weights(w_uq[j], w_dkv[j], w_ukv[j])
            ct, st = _rope_lane_tables(s)
            q, k, v = _mla_prep(x2, s, mix_norm_o[j].reshape(1, d), w_dq[j].astype(BF16),
                                q_norm_g[j].reshape(1, -1), wuq, wuqs, wdkv, kv_norm_g[j].reshape(1, -1),
                                wk, wv, ct, st)
            o = _mla_attention(q.reshape(b, s, -1), k.reshape(b, s, -1), v.reshape(b, s, -1))
            acts, w_out = [o.reshape(t, -1)], w_o_mla[j].astype(BF16)
        x = _mix_ffn(acts, w_out, x2, s, ffn_norm_g[i], w_ffn_gate[i].astype(BF16), w_ffn_val[i].astype(BF16),
                     ffn_conv_w[i], ffn_conv_b[i], w_ffn_down[i].astype(BF16), final_norm_g,
                     final_norm=(i == depth - 1)).reshape(b, s, d)
    return x
```

```python
import functools
import math

import numpy as np
import jax
import jax.numpy as jnp
from jax import lax
from jax.experimental import pallas as pl
from jax.experimental.pallas import tpu as pltpu

F32 = jnp.float32
BF16 = jnp.bfloat16

EPS = 1e-6
NEG_INF = -1e30
GRID_W = 64
HEAD_DIM = 64
DIFF_HEADS = 4
NA_HEADS = 8
NA_WIN_H = 8
NA_WIN_W = 16
MLA_HEADS = 16
MLA_NOPE = 64
MLA_ROPE = 32
MLA_V = 64
MLA_KV_RANK = 256
ROPE_THETA = 10000.0

LANES = 128
VMEM_LIMIT = 56 * 1024 * 1024
NT_DIMS = (((1,), (1,)), ((), ()))
LOG2E = math.log2(math.e)
SQRT_HALF = np.float32(math.sqrt(0.5))
SOFTMAX_SCALE = HEAD_DIM ** -0.5 * LOG2E
MLA_SOFTMAX_SCALE = (MLA_NOPE + MLA_ROPE) ** -0.5 * LOG2E
BOUND_SLACK = 1.0 + 2.0 ** -6
DENOM_MIN = 2.0 ** -60
MLA_AUG_LANE = MLA_NOPE + MLA_ROPE
MLA_VT_ROWS = 80


def _cparams(*sem):
    return pltpu.CompilerParams(dimension_semantics=sem, vmem_limit_bytes=VMEM_LIMIT)


def _rms(xf, g):
    ms = jnp.mean(xf * xf, axis=-1, keepdims=True)
    return xf * lax.rsqrt(ms + EPS) * g


def _dot(a, b):
    return jnp.dot(a, b, preferred_element_type=F32)


def _low_lanes():
    return lax.broadcasted_iota(jnp.int32, (1, LANES), 1) < HEAD_DIM


def _exp_weights(t):
    return jnp.exp2(t - jnp.max(t, axis=-1, keepdims=True)).astype(BF16)


def _fill_ones_aug(vaug_ref, v_ref, n_blocks):
    ones = jnp.ones((v_ref.shape[1], LANES), BF16)
    for p in range(n_blocks):
        vaug_ref[:, 2 * p * LANES:(2 * p + 1) * LANES] = v_ref[0, :, p * LANES:(p + 1) * LANES]
        vaug_ref[:, (2 * p + 1) * LANES:(2 * p + 2) * LANES] = ones


def _norm_proj_body(x_ref, g_ref, w_ref, cs_ref, o_ref, *, n_chunk):
    hn = _rms(x_ref[...], g_ref[...]).astype(BF16)
    for j in range(0, o_ref.shape[1], n_chunk):
        o_ref[:, j:j + n_chunk] = (_dot(hn, w_ref[:, j:j + n_chunk]) * cs_ref[:, j:j + n_chunk]).astype(o_ref.dtype)


def _norm_proj(x2, g, w, col_scale, *, tm=512, n_chunk=768):
    t, d = x2.shape
    n = w.shape[1]
    return pl.pallas_call(
        functools.partial(_norm_proj_body, n_chunk=n_chunk),
        grid=(t // tm,),
        in_specs=[pl.BlockSpec((tm, d), lambda i: (i, 0)),
                  pl.BlockSpec((1, d), lambda i: (0, 0)),
                  pl.BlockSpec((d, n), lambda i: (0, 0)),
                  pl.BlockSpec((1, n), lambda i: (0, 0))],
        out_specs=pl.BlockSpec((tm, n), lambda i: (i, 0)),
        out_shape=jax.ShapeDtypeStruct((t, n), BF16),
        compiler_params=_cparams("parallel"),
        name="norm_proj",
    )(x2, g.reshape(1, d), w, col_scale)


def _diff_attn_body(lvec_ref, q_ref, k_ref, v_ref, g_ref, o_ref, vaug_ref, kmax_ref, t_ref, *, tq, lam_init):
    qi = pl.program_id(1)
    lo = _low_lanes()

    @pl.when(qi == 0)
    def _():
        _fill_ones_aug(vaug_ref, v_ref, DIFF_HEADS)
        for h in range(DIFF_HEADS):
            kf = k_ref[0, :, h * LANES:(h + 1) * LANES].astype(F32)
            k2 = kf * kf
            n1 = jnp.max(jnp.sum(jnp.where(lo, k2, 0.0), axis=-1, keepdims=True), axis=0, keepdims=True)
            n2 = jnp.max(jnp.sum(jnp.where(lo, 0.0, k2), axis=-1, keepdims=True), axis=0, keepdims=True)
            kmax_ref[2 * h] = jnp.broadcast_to(jnp.sqrt(n1), kmax_ref.shape[1:])
            kmax_ref[2 * h + 1] = jnp.broadcast_to(jnp.sqrt(n2), kmax_ref.shape[1:])

    s_len = k_ref.shape[1]
    row = qi * tq + lax.broadcasted_iota(jnp.int32, (tq, s_len), 0)
    col = lax.broadcasted_iota(jnp.int32, (tq, s_len), 1)
    dist = jnp.abs(row - col).astype(F32)
    lv = lvec_ref[...]
    lam = (jnp.exp(jnp.sum(lv[0:1] * lv[1:2], axis=-1, keepdims=True))
           - jnp.exp(jnp.sum(lv[2:3] * lv[3:4], axis=-1, keepdims=True)) + lam_init)

    def maps(h):
        q = q_ref[0, :, h * LANES:(h + 1) * LANES]
        k = k_ref[0, :, h * LANES:(h + 1) * LANES]
        zero = jnp.zeros_like(q)
        bias = (-LOG2E * 2.0 ** (-8.0 * (h + 1) / DIFF_HEADS)) * dist
        s1 = lax.dot_general(jnp.where(lo, q, zero), k, NT_DIMS, preferred_element_type=F32)
        s2 = lax.dot_general(jnp.where(lo, zero, q), k, NT_DIMS, preferred_element_type=F32)
        return q, s1, s2, bias

    def finish(h, o1, o2):
        o = o1[:, :LANES] / o1[:, LANES:] - lam * (o2[:, :LANES] / o2[:, LANES:])
        o_ref[0, :, h * LANES:(h + 1) * LANES] = (_rms(o, g_ref[...]) * (1.0 - lam_init)).astype(o_ref.dtype)

    lmin = None
    for h in range(DIFF_HEADS):
        q, s1, s2, bias = maps(h)
        q2 = jnp.square(q.astype(F32))
        b1 = (jnp.sqrt(jnp.sum(jnp.where(lo, q2, 0.0), axis=-1, keepdims=True))
              * (kmax_ref[2 * h][0:1, 0:1] * BOUND_SLACK))
        b2 = (jnp.sqrt(jnp.sum(jnp.where(lo, 0.0, q2), axis=-1, keepdims=True))
              * (kmax_ref[2 * h + 1][0:1, 0:1] * BOUND_SLACK))
        vaug = vaug_ref[:, 2 * h * LANES:(2 * h + 2) * LANES]
        o1 = _dot(jnp.exp2(s1 + (bias - b1)).astype(BF16), vaug)
        o2 = _dot(jnp.exp2(s2 + (bias - b2)).astype(BF16), vaug)
        finish(h, o1, o2)
        l = jnp.minimum(o1[:, LANES:], o2[:, LANES:])
        lmin = l if lmin is None else jnp.minimum(lmin, l)

    @pl.when(jnp.logical_not(jnp.min(lmin) >= DENOM_MIN))
    def _():
        for h in range(DIFF_HEADS):
            _, s1, s2, bias = maps(h)
            t_ref[0] = s1 + bias
            t_ref[1] = s2 + bias
            vaug = vaug_ref[:, 2 * h * LANES:(2 * h + 2) * LANES]
            finish(h, _dot(_exp_weights(t_ref[0]), vaug), _dot(_exp_weights(t_ref[1]), vaug))


def _diff_attention(proj, lvec, subln_g, *, lam_init, tq=512):
    b, s, _ = proj.shape
    w = DIFF_HEADS * LANES
    return pl.pallas_call(
        functools.partial(_diff_attn_body, tq=tq, lam_init=lam_init),
        grid=(b, s // tq),
        in_specs=[pl.BlockSpec((4, HEAD_DIM), lambda bi, qi: (0, 0)),
                  pl.BlockSpec((1, tq, w), lambda bi, qi: (bi, qi, 0)),
                  pl.BlockSpec((1, s, w), lambda bi, qi: (bi, 0, 1)),
                  pl.BlockSpec((1, s, w), lambda bi, qi: (bi, 0, 2)),
                  pl.BlockSpec((1, LANES), lambda bi, qi: (0, 0))],
        out_specs=pl.BlockSpec((1, tq, w), lambda bi, qi: (bi, qi, 0)),
        out_shape=jax.ShapeDtypeStruct((b, s, w), BF16),
        scratch_shapes=[pltpu.VMEM((s, 2 * w), BF16), pltpu.VMEM((2 * DIFF_HEADS, 8, LANES), F32),
                        pltpu.VMEM((2, tq, s), F32)],
        compiler_params=_cparams("parallel", "arbitrary"),
        name="diff_attn",
    )(lvec, proj, proj, proj, subln_g.reshape(1, LANES))


def _na_bias_table(rpb, kh):
    qc = np.arange(GRID_W)[:, None]
    kc = np.arange(GRID_W)[None, :]
    wstart = np.clip(qc - NA_WIN_W // 2, 0, GRID_W - NA_WIN_W)
    valid = (kc >= wstart) & (kc < wstart + NA_WIN_W)
    dc = kc - qc + NA_WIN_W - 1
    onehot = ((dc[None] == np.arange(2 * NA_WIN_W - 1)[:, None, None]) & valid[None]).astype(np.float32)
    toep = jnp.einsum("hrc,cqk->hrqk", rpb.astype(F32) * LOG2E, onehot, precision=lax.Precision.HIGHEST)
    toep = jnp.where(valid[None, None], toep, NEG_INF)
    per_d = [toep[:, NA_WIN_H - 1 - d:NA_WIN_H - 1 - d + kh].transpose(0, 2, 1, 3) for d in range(kh)]
    return jnp.stack(per_d, axis=1).reshape(rpb.shape[0], kh, GRID_W, kh * GRID_W)


def _na_body(q_ref, k_ref, v_ref, bias_ref, o_ref, vaug_ref, *, rows_n, kh):
    lo = _low_lanes()
    _fill_ones_aug(vaug_ref, v_ref, 1)
    for r in range(rows_n):
        rs = min(max(r - kh // 2, 0), rows_n - kh)
        q = q_ref[0, r * GRID_W:(r + 1) * GRID_W, :]
        kwin = k_ref[0, rs * GRID_W:(rs + kh) * GRID_W, :]
        vwin = vaug_ref[rs * GRID_W:(rs + kh) * GRID_W, :]
        zero = jnp.zeros_like(q)
        q2 = jnp.concatenate([jnp.where(lo, q, zero), jnp.where(lo, zero, q)], axis=0)
        bias = bias_ref[:, r - rs].reshape(2 * GRID_W, kh * GRID_W)
        t = lax.dot_general(q2, kwin, NT_DIMS, preferred_element_type=F32) + bias
        o2 = _dot(_exp_weights(t), vwin)
        o2 = o2[:, :LANES] / o2[:, LANES:]
        o_ref[0, r * GRID_W:(r + 1) * GRID_W, :] = jnp.where(lo, o2[:GRID_W], o2[GRID_W:]).astype(o_ref.dtype)


def _na_attention(proj, rpb, *, col0):
    b, s, _ = proj.shape
    rows_n = s // GRID_W
    kh = min(NA_WIN_H, rows_n)
    pairs = NA_HEADS // 2
    bias = _na_bias_table(rpb, kh)
    return pl.pallas_call(
        functools.partial(_na_body, rows_n=rows_n, kh=kh),
        grid=(b, pairs),
        in_specs=[pl.BlockSpec((1, s, LANES), lambda bi, p: (bi, 0, col0 + p)),
                  pl.BlockSpec((1, s, LANES), lambda bi, p: (bi, 0, col0 + pairs + p)),
                  pl.BlockSpec((1, s, LANES), lambda bi, p: (bi, 0, col0 + 2 * pairs + p)),
                  pl.BlockSpec((2, kh, GRID_W, kh * GRID_W), lambda bi, p: (p, 0, 0, 0))],
        out_specs=pl.BlockSpec((1, s, LANES), lambda bi, p: (bi, 0, p)),
        out_shape=jax.ShapeDtypeStruct((b, s, pairs * LANES), BF16),
        scratch_shapes=[pltpu.VMEM((s, 2 * LANES), BF16)],
        compiler_params=_cparams("parallel", "parallel"),
        name="na_attn",
    )(proj, proj, proj, bias)


HALO = 16


def _mix_ffn_body(*refs, n_act, tiles_per_seq, final_norm):
    acts = refs[:3 * n_act]
    wos = refs[3 * n_act:4 * n_act]
    x_ref, xp_ref, xn_ref, g_ref, wg_ref, wv_ref, cw_ref, cb_ref, wd_ref, fg_ref, o_ref = refs[4 * n_act:]
    pos = pl.program_id(0) % tiles_per_seq
    tm = x_ref.shape[0]
    xe = jnp.concatenate([xp_ref[...], x_ref[...], xn_ref[...]], axis=0)
    for k in range(n_act):
        a_ref, ap_ref, an_ref = acts[3 * k:3 * k + 3]
        xe = xe + _dot(jnp.concatenate([ap_ref[...], a_ref[...], an_ref[...]], axis=0), wos[k][...])
    row = lax.broadcasted_iota(jnp.int32, (tm + 2 * HALO, 1), 0)
    first_kept = jnp.where(pos == 0, HALO, 0)
    end_kept = jnp.where(pos == tiles_per_seq - 1, HALO + tm, tm + 2 * HALO)
    xe = jnp.where((row >= first_kept) & (row < end_kept), xe, 0.0)
    y = _rms(xe, g_ref[...])
    a_ext = _dot(y.astype(BF16), wg_ref[...])
    cw = cw_ref[...]
    a = (cw[0:1] * a_ext[HALO - 1:HALO - 1 + tm] + cw[1:2] * a_ext[HALO:HALO + tm]
         + cw[2:3] * a_ext[HALO + 1:HALO + 1 + tm] + cb_ref[...])
    gate = 0.5 * a * (1.0 + lax.erf(a * SQRT_HALF))
    val = _dot(y[HALO:HALO + tm].astype(BF16), wv_ref[...])
    out = xe[HALO:HALO + tm] + _dot((gate * val).astype(BF16), wd_ref[...])
    o_ref[...] = _rms(out, fg_ref[...]) if final_norm else out


def _mix_ffn(acts, w_out, x2, s, g, w_gate, w_val, conv_w, conv_b, w_down, final_g, *, final_norm, tm=512):
    t, d = x2.shape
    dff = w_gate.shape[1]
    wa = acts[0].shape[1]
    assert all(a.shape[1] == wa for a in acts) and wa * len(acts) == w_out.shape[0]
    hb = tm // HALO
    last = t // HALO - 1

    def tiles(width):
        return [pl.BlockSpec((tm, width), lambda i: (i, 0)),
                pl.BlockSpec((HALO, width), lambda i: (jnp.maximum(i * hb - 1, 0), 0)),
                pl.BlockSpec((HALO, width), lambda i: (jnp.minimum((i + 1) * hb, last), 0))]

    def const(shape, row_block=0):
        return pl.BlockSpec(shape, lambda i: (row_block, 0), pipeline_mode=pl.Buffered(1))

    consts = (g.reshape(1, d), w_gate, w_val, conv_w, conv_b.reshape(1, dff), w_down, final_g.reshape(1, d))
    act_args = [a for act in acts for a in (act, act, act)]
    act_specs = [sp for _ in acts for sp in tiles(wa)]
    return pl.pallas_call(
        functools.partial(_mix_ffn_body, n_act=len(acts), tiles_per_seq=s // tm, final_norm=final_norm),
        grid=(t // tm,),
        in_specs=(act_specs + [const((wa, d), k) for k in range(len(acts))] + tiles(d)
                  + [const(c.shape) for c in consts]),
        out_specs=pl.BlockSpec((tm, d), lambda i: (i, 0)),
        out_shape=jax.ShapeDtypeStruct((t, d), F32),
        compiler_params=_cparams("parallel"),
        name="mix_ffn",
    )(*act_args, *([w_out] * len(acts)), x2, x2, x2, *consts)


def _mla_prep_body(x_ref, g_ref, wdq_ref, qg_ref, wuq_ref, wuqs_ref, wdkv_ref, kvg_ref, wk_ref, wvt_ref,
                   ct_ref, st_ref, q_out, k_out, vt_out):
    hn = _rms(x_ref[...], g_ref[...]).astype(BF16)
    cq = _rms(_dot(hn, wdq_ref[...]), qg_ref[...]).astype(BF16)
    kva = _dot(hn, wdkv_ref[...])
    ckv = _rms(kva[:, :MLA_KV_RANK], kvg_ref[...]).astype(BF16)
    ct = ct_ref[...]
    st = st_ref[...]
    kr = kva[:, MLA_KV_RANK:MLA_KV_RANK + LANES]
    kr_sw = kva[:, MLA_KV_RANK + LANES:]
    roped = kr * ct[:, :LANES] + kr_sw * st[:, :LANES]
    shift_lane = lax.broadcasted_iota(jnp.int32, (1, LANES), 1) == MLA_AUG_LANE
    roped = jnp.where(shift_lane, 1.0, roped)
    roped2 = jnp.concatenate([roped, roped], axis=1)
    w2 = 2 * LANES
    for j in range(0, q_out.shape[1], w2):
        qh = _dot(cq, wuq_ref[:, j:j + w2])
        qs = _dot(cq, wuqs_ref[:, j:j + w2])
        q_out[:, j:j + w2] = ((qh * ct + qs * st) * MLA_SOFTMAX_SCALE).astype(q_out.dtype)
        k_out[:, j:j + w2] = (_dot(ckv, wk_ref[:, j:j + w2]) + roped2).astype(k_out.dtype)
    vt = lax.dot_general(wvt_ref[...], ckv, NT_DIMS, preferred_element_type=F32)
    ones_row = lax.broadcasted_iota(jnp.int32, (vt.shape[0], 1), 0) % MLA_VT_ROWS == MLA_V
    vt_out[0] = jnp.where(ones_row, 1.0, vt).astype(vt_out.dtype)


def _mla_prep(x2, s, g, wdq, qg, wuq, wuqs, wdkv, kvg, wk, wvt, ct, st, *, tm=512):
    t, d = x2.shape
    nblk = s // tm
    full = lambda a: pl.BlockSpec(a.shape, lambda i: (0, 0))
    hw = MLA_HEADS * LANES
    return pl.pallas_call(
        _mla_prep_body,
        grid=(t // tm,),
        in_specs=[pl.BlockSpec((tm, d), lambda i: (i, 0)), full(g), full(wdq), full(qg), full(wuq), full(wuqs),
                  full(wdkv), full(kvg), full(wk), full(wvt),
                  pl.BlockSpec((tm, 2 * LANES), lambda i: (i % nblk, 0)),
                  pl.BlockSpec((tm, 2 * LANES), lambda i: (i % nblk, 0))],
        out_specs=[pl.BlockSpec((tm, hw), lambda i: (i, 0)),
                   pl.BlockSpec((tm, hw), lambda i: (i, 0)),
                   pl.BlockSpec((1, MLA_HEADS * MLA_VT_ROWS, tm), lambda i: (i // nblk, 0, i % nblk))],
        out_shape=[jax.ShapeDtypeStruct((t, hw), BF16),
                   jax.ShapeDtypeStruct((t, hw), BF16),
                   jax.ShapeDtypeStruct((t // s, MLA_HEADS * MLA_VT_ROWS, s), BF16)],
        compiler_params=_cparams("parallel"),
        name="mla_prep",
    )(x2, g, wdq, qg, wuq, wuqs, wdkv, kvg, wk, wvt, ct, st)


def _mla_attn_body(q_ref, k_ref, vt_ref, o_ref, kmax_ref, t_ref, *, heads):
    shift_lane = lax.broadcasted_iota(jnp.int32, (1, LANES), 1) == MLA_AUG_LANE

    @pl.when(pl.program_id(2) == 0)
    def _():
        for hh in range(heads):
            kf = k_ref[0, :, hh * LANES:(hh + 1) * LANES].astype(F32)
            n = jnp.max(jnp.sum(jnp.where(shift_lane, 0.0, kf * kf), axis=-1, keepdims=True), axis=0, keepdims=True)
            kmax_ref[hh] = jnp.broadcast_to(jnp.sqrt(n), kmax_ref.shape[1:])

    def values(hh, et, halves):
        ot = _dot(vt_ref[0, hh * MLA_VT_ROWS:(hh + 1) * MLA_VT_ROWS, :], et)
        denom = ot[MLA_V:MLA_V + 1]
        halves.append(ot[:MLA_V] / denom)
        if hh % 2 == 1:
            p = hh // 2
            o_ref[0, :, p * LANES:(p + 1) * LANES] = jnp.concatenate(halves, axis=0).T.astype(o_ref.dtype)
            halves.clear()
        return denom

    lmin = None
    halves = []
    for hh in range(heads):
        q = q_ref[0, :, hh * LANES:(hh + 1) * LANES]
        k = k_ref[0, :, hh * LANES:(hh + 1) * LANES]
        qn = jnp.sqrt(jnp.sum(jnp.square(q.astype(F32)), axis=-1, keepdims=True))
        shift = (qn * (kmax_ref[hh][0:1, 0:1] * -BOUND_SLACK)).astype(BF16)
        tt = lax.dot_general(k, jnp.where(shift_lane, shift, q), NT_DIMS, preferred_element_type=F32)
        denom = values(hh, jnp.exp2(tt).astype(BF16), halves)
        lmin = denom if lmin is None else jnp.minimum(lmin, denom)

    @pl.when(jnp.logical_not(jnp.min(lmin) >= DENOM_MIN))
    def _():
        halves = []
        for hh in range(heads):
            q = q_ref[0, :, hh * LANES:(hh + 1) * LANES]
            k = k_ref[0, :, hh * LANES:(hh + 1) * LANES]
            t_ref[...] = lax.dot_general(k, q, NT_DIMS, preferred_element_type=F32)
            m = jnp.max(t_ref[...], axis=0, keepdims=True)
            values(hh, jnp.exp2(t_ref[...] - m).astype(BF16), halves)


def _mla_attention(q, k, vt, *, tq=512, heads=8):
    b, s, _ = q.shape
    groups = MLA_HEADS // heads
    return pl.pallas_call(
        functools.partial(_mla_attn_body, heads=heads),
        grid=(b, groups, s // tq),
        in_specs=[pl.BlockSpec((1, tq, heads * LANES), lambda bi, g, qi: (bi, qi, g)),
                  pl.BlockSpec((1, s, heads * LANES), lambda bi, g, qi: (bi, 0, g)),
                  pl.BlockSpec((1, heads * MLA_VT_ROWS, s), lambda bi, g, qi: (bi, g, 0))],
        out_specs=pl.BlockSpec((1, tq, heads * MLA_V), lambda bi, g, qi: (bi, qi, g)),
        out_shape=jax.ShapeDtypeStruct((b, s, MLA_HEADS * MLA_V), BF16),
        scratch_shapes=[pltpu.VMEM((heads, 8, LANES), F32), pltpu.VMEM((s, tq), F32)],
        compiler_params=_cparams("parallel", "parallel", "arbitrary"),
        name="mla_attn",
    )(q, k, vt)


def _mla_weights(w_uq, w_dkv, w_ukv):
    f = MLA_NOPE + MLA_ROPE
    half = MLA_ROPE // 2
    r = w_uq.shape[0]
    uq = w_uq.reshape(r, MLA_HEADS, f)
    zq = jnp.zeros((r, MLA_HEADS, LANES - f), w_uq.dtype)
    wuq = jnp.concatenate([uq, zq], axis=-1).reshape(r, MLA_HEADS * LANES)
    wuqs = jnp.concatenate([jnp.zeros((r, MLA_HEADS, MLA_NOPE), w_uq.dtype),
                            uq[..., MLA_NOPE + half:], uq[..., MLA_NOPE:MLA_NOPE + half], zq],
                           axis=-1).reshape(r, MLA_HEADS * LANES)
    d = w_dkv.shape[0]
    kr = w_dkv[:, MLA_KV_RANK:]
    z64 = jnp.zeros((d, MLA_NOPE), w_dkv.dtype)
    z32 = jnp.zeros((d, LANES - f), w_dkv.dtype)
    wdkv = jnp.concatenate([w_dkv[:, :MLA_KV_RANK], z64, kr, z32,
                            z64, kr[:, half:], kr[:, :half], z32], axis=-1)
    c = w_ukv.shape[0]
    ukv = w_ukv.reshape(c, MLA_HEADS, MLA_NOPE + MLA_V)
    wk = jnp.concatenate([ukv[..., :MLA_NOPE], jnp.zeros((c, MLA_HEADS, LANES - MLA_NOPE), w_ukv.dtype)],
                         axis=-1).reshape(c, MLA_HEADS * LANES)
    wvt = jnp.concatenate([ukv[..., MLA_NOPE:].transpose(1, 2, 0),
                           jnp.zeros((MLA_HEADS, MLA_VT_ROWS - MLA_V, c), w_ukv.dtype)], axis=1)
    wvt = wvt.reshape(MLA_HEADS * MLA_VT_ROWS, c)
    return wuq.astype(BF16), wuqs.astype(BF16), wdkv.astype(BF16), wk.astype(BF16), wvt.astype(BF16)


def _rope_lane_tables(s):
    inv = ROPE_THETA ** (-jnp.arange(0, MLA_ROPE, 2, dtype=F32) / MLA_ROPE)
    ang = jnp.arange(s, dtype=F32)[:, None] * inv[None, :]
    cos, sin = jnp.cos(ang), jnp.sin(ang)
    one = jnp.ones((s, MLA_NOPE), F32)
    z64 = jnp.zeros((s, MLA_NOPE), F32)
    z32 = jnp.zeros((s, LANES - MLA_NOPE - MLA_ROPE), F32)
    ct = jnp.concatenate([one, cos, cos, z32], axis=-1)
    st = jnp.concatenate([z64, -sin, sin, z32], axis=-1)
    return jnp.tile(ct, (1, 2)), jnp.tile(st, (1, 2))


def kernel(x, mix_norm_e, w_in_e, diff_lq1, diff_lk1, diff_lq2, diff_lk2, diff_subln_g, na_rpb, w_out_e,
           mix_norm_o, w_dq, q_norm_g, w_uq, w_dkv, kv_norm_g, w_ukv, w_o_mla, ffn_norm_g, w_ffn_gate,
           w_ffn_val, ffn_conv_w, ffn_conv_b, w_ffn_down, final_norm_g):
    b, s, d = x.shape
    t = b * s
    depth = ffn_norm_g.shape[0]
    for i in range(depth):
        j = i // 2
        x2 = x.reshape(t, d)
        if i % 2 == 0:
            lam_init = 0.8 - 0.6 * math.exp(-0.3 * i)
            wq = DIFF_HEADS * LANES
            col_scale = np.ones((1, w_in_e.shape[-1]), np.float32)
            col_scale[:, :wq] = SOFTMAX_SCALE
            col_scale[:, 3 * wq:3 * wq + NA_HEADS * HEAD_DIM] = SOFTMAX_SCALE
            proj = _norm_proj(x2, mix_norm_e[j], w_in_e[j].astype(BF16), jnp.asarray(col_scale)).reshape(b, s, -1)
            lvec = jnp.stack([diff_lq1[j], diff_lk1[j], diff_lq2[j], diff_lk2[j]]).astype(F32)
            a = _diff_attention(proj, lvec, diff_subln_g[j], lam_init=lam_init)
            nb = _na_attention(proj, na_rpb[j], col0=3 * DIFF_HEADS)
            acts, w_out = [a.reshape(t, -1), nb.reshape(t, -1)], w_out_e[j].astype(BF16)
        else:
            wuq, wuqs, wdkv, wk, wvt = _mla_weights(w_uq[j], w_dkv[j], w_ukv[j])
            ct, st = _rope_lane_tables(s)
            q, k, vt = _mla_prep(x2, s, mix_norm_o[j].reshape(1, d), w_dq[j].astype(BF16),
                                 q_norm_g[j].reshape(1, -1), wuq, wuqs, wdkv, kv_norm_g[j].reshape(1, -1),
                                 wk, wvt, ct, st)
            o = _mla_attention(q.reshape(b, s, -1), k.reshape(b, s, -1), vt)
            acts, w_out = [o.reshape(t, -1)], w_o_mla[j].astype(BF16)
        x = _mix_ffn(acts, w_out, x2, s, ffn_norm_g[i], w_ffn_gate[i].astype(BF16), w_ffn_val[i].astype(BF16),
                     ffn_conv_w[i], ffn_conv_b[i], w_ffn_down[i].astype(BF16), final_norm_g,
                     final_norm=(i == depth - 1)).reshape(b, s, d)
    return x
```

```python
import functools
import math

import numpy as np
import jax
import jax.numpy as jnp
from jax import lax
from jax.experimental import pallas as pl
from jax.experimental.pallas import tpu as pltpu

F32 = jnp.float32
BF16 = jnp.bfloat16

EPS = 1e-6
NEG_INF = -1e30
GRID_W = 64
HEAD_DIM = 64
DIFF_HEADS = 4
NA_HEADS = 8
NA_WIN_H = 8
NA_WIN_W = 16
MLA_HEADS = 16
MLA_NOPE = 64
MLA_ROPE = 32
MLA_V = 64
MLA_KV_RANK = 256
ROPE_THETA = 10000.0

LANES = 128
VMEM_LIMIT = 56 * 1024 * 1024
NT_DIMS = (((1,), (1,)), ((), ()))
LOG2E = math.log2(math.e)
SQRT_HALF = np.float32(math.sqrt(0.5))
SOFTMAX_SCALE = HEAD_DIM ** -0.5 * LOG2E
MLA_SOFTMAX_SCALE = (MLA_NOPE + MLA_ROPE) ** -0.5 * LOG2E
BOUND_SLACK = 1.0 + 2.0 ** -6
DENOM_MIN = 2.0 ** -60
MLA_AUG_LANE = MLA_NOPE + MLA_ROPE
MLA_HEADS_PER_STEP = 8
MLA_VT_ROWS = 80


def _cparams(*sem):
    return pltpu.CompilerParams(dimension_semantics=sem, vmem_limit_bytes=VMEM_LIMIT)


def _rms(xf, g):
    ms = jnp.mean(xf * xf, axis=-1, keepdims=True)
    return xf * lax.rsqrt(ms + EPS) * g


def _dot(a, b):
    return jnp.dot(a, b, preferred_element_type=F32)


def _low_lanes():
    return lax.broadcasted_iota(jnp.int32, (1, LANES), 1) < HEAD_DIM


def _exp_weights(t):
    return jnp.exp2(t - jnp.max(t, axis=-1, keepdims=True)).astype(BF16)


def _fill_ones_aug(vaug_ref, v_ref, n_blocks):
    ones = jnp.ones((v_ref.shape[1], LANES), BF16)
    for p in range(n_blocks):
        vaug_ref[:, 2 * p * LANES:(2 * p + 1) * LANES] = v_ref[0, :, p * LANES:(p + 1) * LANES]
        vaug_ref[:, (2 * p + 1) * LANES:(2 * p + 2) * LANES] = ones


def _norm_proj_body(x_ref, g_ref, w_ref, cs_ref, o_ref, *, n_chunk):
    hn = _rms(x_ref[...], g_ref[...]).astype(BF16)
    for j in range(0, o_ref.shape[1], n_chunk):
        o_ref[:, j:j + n_chunk] = (_dot(hn, w_ref[:, j:j + n_chunk]) * cs_ref[:, j:j + n_chunk]).astype(o_ref.dtype)


def _norm_proj(x2, g, w, col_scale, *, tm=512, n_chunk=768):
    t, d = x2.shape
    n = w.shape[1]
    return pl.pallas_call(
        functools.partial(_norm_proj_body, n_chunk=n_chunk),
        grid=(t // tm,),
        in_specs=[pl.BlockSpec((tm, d), lambda i: (i, 0)),
                  pl.BlockSpec((1, d), lambda i: (0, 0)),
                  pl.BlockSpec((d, n), lambda i: (0, 0)),
                  pl.BlockSpec((1, n), lambda i: (0, 0))],
        out_specs=pl.BlockSpec((tm, n), lambda i: (i, 0)),
        out_shape=jax.ShapeDtypeStruct((t, n), BF16),
        compiler_params=_cparams("parallel"),
        name="norm_proj",
    )(x2, g.reshape(1, d), w, col_scale)


def _diff_attn_body(lvec_ref, q_ref, k_ref, v_ref, g_ref, o_ref, vaug_ref, kmax_ref, t_ref, *, tq, lam_init):
    qi = pl.program_id(1)
    lo = _low_lanes()

    @pl.when(qi == 0)
    def _():
        _fill_ones_aug(vaug_ref, v_ref, DIFF_HEADS)
        for h in range(DIFF_HEADS):
            kf = k_ref[0, :, h * LANES:(h + 1) * LANES].astype(F32)
            k2 = kf * kf
            n1 = jnp.max(jnp.sum(jnp.where(lo, k2, 0.0), axis=-1, keepdims=True), axis=0, keepdims=True)
            n2 = jnp.max(jnp.sum(jnp.where(lo, 0.0, k2), axis=-1, keepdims=True), axis=0, keepdims=True)
            kmax_ref[2 * h] = jnp.broadcast_to(jnp.sqrt(n1), kmax_ref.shape[1:])
            kmax_ref[2 * h + 1] = jnp.broadcast_to(jnp.sqrt(n2), kmax_ref.shape[1:])

    s_len = k_ref.shape[1]
    row = qi * tq + lax.broadcasted_iota(jnp.int32, (tq, s_len), 0)
    col = lax.broadcasted_iota(jnp.int32, (tq, s_len), 1)
    dist = jnp.abs(row - col).astype(F32)
    lv = lvec_ref[...]
    lam = (jnp.exp(jnp.sum(lv[0:1] * lv[1:2], axis=-1, keepdims=True))
           - jnp.exp(jnp.sum(lv[2:3] * lv[3:4], axis=-1, keepdims=True)) + lam_init)

    def maps(h):
        q = q_ref[0, :, h * LANES:(h + 1) * LANES]
        k = k_ref[0, :, h * LANES:(h + 1) * LANES]
        zero = jnp.zeros_like(q)
        bias = (-LOG2E * 2.0 ** (-8.0 * (h + 1) / DIFF_HEADS)) * dist
        s1 = lax.dot_general(jnp.where(lo, q, zero), k, NT_DIMS, preferred_element_type=F32)
        s2 = lax.dot_general(jnp.where(lo, zero, q), k, NT_DIMS, preferred_element_type=F32)
        return q, s1, s2, bias

    def finish(h, o1, o2):
        o = o1[:, :LANES] / o1[:, LANES:] - lam * (o2[:, :LANES] / o2[:, LANES:])
        o_ref[0, :, h * LANES:(h + 1) * LANES] = (_rms(o, g_ref[...]) * (1.0 - lam_init)).astype(o_ref.dtype)

    lmin = None
    for h in range(DIFF_HEADS):
        q, s1, s2, bias = maps(h)
        q2 = jnp.square(q.astype(F32))
        b1 = (jnp.sqrt(jnp.sum(jnp.where(lo, q2, 0.0), axis=-1, keepdims=True))
              * (kmax_ref[2 * h][0:1, 0:1] * BOUND_SLACK))
        b2 = (jnp.sqrt(jnp.sum(jnp.where(lo, 0.0, q2), axis=-1, keepdims=True))
              * (kmax_ref[2 * h + 1][0:1, 0:1] * BOUND_SLACK))
        vaug = vaug_ref[:, 2 * h * LANES:(2 * h + 2) * LANES]
        o1 = _dot(jnp.exp2(s1 + (bias - b1)).astype(BF16), vaug)
        o2 = _dot(jnp.exp2(s2 + (bias - b2)).astype(BF16), vaug)
        finish(h, o1, o2)
        l = jnp.minimum(o1[:, LANES:], o2[:, LANES:])
        lmin = l if lmin is None else jnp.minimum(lmin, l)

    @pl.when(jnp.logical_not(jnp.min(lmin) >= DENOM_MIN))
    def _():
        for h in range(DIFF_HEADS):
            _, s1, s2, bias = maps(h)
            t_ref[0] = s1 + bias
            t_ref[1] = s2 + bias
            vaug = vaug_ref[:, 2 * h * LANES:(2 * h + 2) * LANES]
            finish(h, _dot(_exp_weights(t_ref[0]), vaug), _dot(_exp_weights(t_ref[1]), vaug))


def _diff_attention(proj, lvec, subln_g, *, lam_init, tq=512):
    b, s, _ = proj.shape
    w = DIFF_HEADS * LANES
    return pl.pallas_call(
        functools.partial(_diff_attn_body, tq=tq, lam_init=lam_init),
        grid=(b, s // tq),
        in_specs=[pl.BlockSpec((4, HEAD_DIM), lambda bi, qi: (0, 0)),
                  pl.BlockSpec((1, tq, w), lambda bi, qi: (bi, qi, 0)),
                  pl.BlockSpec((1, s, w), lambda bi, qi: (bi, 0, 1)),
                  pl.BlockSpec((1, s, w), lambda bi, qi: (bi, 0, 2)),
                  pl.BlockSpec((1, LANES), lambda bi, qi: (0, 0))],
        out_specs=pl.BlockSpec((1, tq, w), lambda bi, qi: (bi, qi, 0)),
        out_shape=jax.ShapeDtypeStruct((b, s, w), BF16),
        scratch_shapes=[pltpu.VMEM((s, 2 * w), BF16), pltpu.VMEM((2 * DIFF_HEADS, 8, LANES), F32),
                        pltpu.VMEM((2, tq, s), F32)],
        compiler_params=_cparams("parallel", "arbitrary"),
        name="diff_attn",
    )(lvec, proj, proj, proj, subln_g.reshape(1, LANES))


def _na_bias_table(rpb, kh):
    qc = np.arange(GRID_W)[:, None]
    kc = np.arange(GRID_W)[None, :]
    wstart = np.clip(qc - NA_WIN_W // 2, 0, GRID_W - NA_WIN_W)
    valid = (kc >= wstart) & (kc < wstart + NA_WIN_W)
    dc = kc - qc + NA_WIN_W - 1
    onehot = ((dc[None] == np.arange(2 * NA_WIN_W - 1)[:, None, None]) & valid[None]).astype(np.float32)
    toep = jnp.einsum("hrc,cqk->hrqk", rpb.astype(F32) * LOG2E, onehot, precision=lax.Precision.HIGHEST)
    toep = jnp.where(valid[None, None], toep, NEG_INF)
    per_d = [toep[:, NA_WIN_H - 1 - d:NA_WIN_H - 1 - d + kh].transpose(0, 2, 1, 3) for d in range(kh)]
    return jnp.stack(per_d, axis=1).reshape(rpb.shape[0], kh, GRID_W, kh * GRID_W)


def _na_body(q_ref, k_ref, v_ref, bias_ref, o_ref, vaug_ref, *, rows_n, kh):
    lo = _low_lanes()
    _fill_ones_aug(vaug_ref, v_ref, 1)
    for r in range(rows_n):
        rs = min(max(r - kh // 2, 0), rows_n - kh)
        q = q_ref[0, r * GRID_W:(r + 1) * GRID_W, :]
        kwin = k_ref[0, rs * GRID_W:(rs + kh) * GRID_W, :]
        vwin = vaug_ref[rs * GRID_W:(rs + kh) * GRID_W, :]
        zero = jnp.zeros_like(q)
        q2 = jnp.concatenate([jnp.where(lo, q, zero), jnp.where(lo, zero, q)], axis=0)
        bias = bias_ref[:, r - rs].reshape(2 * GRID_W, kh * GRID_W)
        t = lax.dot_general(q2, kwin, NT_DIMS, preferred_element_type=F32) + bias
        o2 = _dot(_exp_weights(t), vwin)
        o2 = o2[:, :LANES] / o2[:, LANES:]
        o_ref[0, r * GRID_W:(r + 1) * GRID_W, :] = jnp.where(lo, o2[:GRID_W], o2[GRID_W:]).astype(o_ref.dtype)


def _na_attention(proj, rpb, *, col0):
    b, s, _ = proj.shape
    rows_n = s // GRID_W
    kh = min(NA_WIN_H, rows_n)
    pairs = NA_HEADS // 2
    bias = _na_bias_table(rpb, kh)
    return pl.pallas_call(
        functools.partial(_na_body, rows_n=rows_n, kh=kh),
        grid=(b, pairs),
        in_specs=[pl.BlockSpec((1, s, LANES), lambda bi, p: (bi, 0, col0 + p)),
                  pl.BlockSpec((1, s, LANES), lambda bi, p: (bi, 0, col0 + pairs + p)),
                  pl.BlockSpec((1, s, LANES), lambda bi, p: (bi, 0, col0 + 2 * pairs + p)),
                  pl.BlockSpec((2, kh, GRID_W, kh * GRID_W), lambda bi, p: (p, 0, 0, 0))],
        out_specs=pl.BlockSpec((1, s, LANES), lambda bi, p: (bi, 0, p)),
        out_shape=jax.ShapeDtypeStruct((b, s, pairs * LANES), BF16),
        scratch_shapes=[pltpu.VMEM((s, 2 * LANES), BF16)],
        compiler_params=_cparams("parallel", "parallel"),
        name="na_attn",
    )(proj, proj, proj, bias)


HALO = 16


def _mix_ffn_body(*refs, n_act, tiles_per_seq, final_norm):
    acts = refs[:3 * n_act]
    wos = refs[3 * n_act:4 * n_act]
    x_ref, xp_ref, xn_ref, g_ref, wg_ref, wv_ref, cw_ref, cb_ref, wd_ref, fg_ref, o_ref = refs[4 * n_act:]
    pos = pl.program_id(0) % tiles_per_seq
    tm = x_ref.shape[0]
    xe = jnp.concatenate([xp_ref[...], x_ref[...], xn_ref[...]], axis=0)
    for k in range(n_act):
        a_ref, ap_ref, an_ref = acts[3 * k:3 * k + 3]
        xe = xe + _dot(jnp.concatenate([ap_ref[...], a_ref[...], an_ref[...]], axis=0), wos[k][...])
    row = lax.broadcasted_iota(jnp.int32, (tm + 2 * HALO, 1), 0)
    first_kept = jnp.where(pos == 0, HALO, 0)
    end_kept = jnp.where(pos == tiles_per_seq - 1, HALO + tm, tm + 2 * HALO)
    xe = jnp.where((row >= first_kept) & (row < end_kept), xe, 0.0)
    y = _rms(xe, g_ref[...])
    a_ext = _dot(y.astype(BF16), wg_ref[...])
    cw = cw_ref[...]
    a = (cw[0:1] * a_ext[HALO - 1:HALO - 1 + tm] + cw[1:2] * a_ext[HALO:HALO + tm]
         + cw[2:3] * a_ext[HALO + 1:HALO + 1 + tm] + cb_ref[...])
    gate = 0.5 * a * (1.0 + lax.erf(a * SQRT_HALF))
    val = _dot(y[HALO:HALO + tm].astype(BF16), wv_ref[...])
    out = xe[HALO:HALO + tm] + _dot((gate * val).astype(BF16), wd_ref[...])
    o_ref[...] = _rms(out, fg_ref[...]) if final_norm else out


def _mix_ffn(acts, w_out, x2, s, g, w_gate, w_val, conv_w, conv_b, w_down, final_g, *, layer, final_norm, tm=512):
    t, d = x2.shape
    dff = w_gate.shape[2]
    wa = acts[0].shape[1]
    assert all(a.shape[1] == wa for a in acts) and wa * len(acts) == w_out.shape[0]
    hb = tm // HALO
    last = t // HALO - 1

    def tiles(width):
        return [pl.BlockSpec((tm, width), lambda i: (i, 0)),
                pl.BlockSpec((HALO, width), lambda i: (jnp.maximum(i * hb - 1, 0), 0)),
                pl.BlockSpec((HALO, width), lambda i: (jnp.minimum((i + 1) * hb, last), 0))]

    def const(shape, row_block=0):
        return pl.BlockSpec(shape, lambda i: (row_block, 0), pipeline_mode=pl.Buffered(1))

    def slab(w):
        return pl.BlockSpec((None,) + w.shape[1:], lambda i: (layer, 0, 0), pipeline_mode=pl.Buffered(1))

    consts = (g.reshape(1, d), w_gate, w_val, conv_w, conv_b.reshape(1, dff), w_down, final_g.reshape(1, d))
    act_args = [a for act in acts for a in (act, act, act)]
    act_specs = [sp for _ in acts for sp in tiles(wa)]
    return pl.pallas_call(
        functools.partial(_mix_ffn_body, n_act=len(acts), tiles_per_seq=s // tm, final_norm=final_norm),
        grid=(t // tm,),
        in_specs=(act_specs + [const((wa, d), k) for k in range(len(acts))] + tiles(d)
                  + [slab(c) if c.ndim == 3 else const(c.shape) for c in consts]),
        out_specs=pl.BlockSpec((tm, d), lambda i: (i, 0)),
        out_shape=jax.ShapeDtypeStruct((t, d), F32),
        compiler_params=_cparams("parallel"),
        name="mix_ffn",
    )(*act_args, *([w_out] * len(acts)), x2, x2, x2, *consts)


def _mla_prep_body(x_ref, g_ref, wdq_ref, qg_ref, wuq_ref, wuqs_ref, wdkv_ref, kvg_ref, wk_ref, wvt_ref,
                   ct_ref, st_ref, q_out, k_out, vt_out, kn_out, *, heads_per_group):
    hn = _rms(x_ref[...], g_ref[...]).astype(BF16)
    cq = _rms(_dot(hn, wdq_ref[...]), qg_ref[...]).astype(BF16)
    kva = _dot(hn, wdkv_ref[...])
    ckv = _rms(kva[:, :MLA_KV_RANK], kvg_ref[...]).astype(BF16)
    ct = ct_ref[...]
    st = st_ref[...]
    kr = kva[:, MLA_KV_RANK:MLA_KV_RANK + LANES]
    kr_sw = kva[:, MLA_KV_RANK + LANES:]
    roped = kr * ct[:, :LANES] + kr_sw * st[:, :LANES]
    shift_lane = lax.broadcasted_iota(jnp.int32, (1, LANES), 1) == MLA_AUG_LANE
    roped = jnp.where(shift_lane, 1.0, roped)
    roped2 = jnp.concatenate([roped, roped], axis=1)
    w2 = 2 * LANES
    lane = lax.broadcasted_iota(jnp.int32, (1, LANES), 1)
    kn_rows = [jnp.zeros((1, LANES), F32) for _ in range(MLA_HEADS // heads_per_group)]
    for j in range(0, q_out.shape[1], w2):
        qh = _dot(cq, wuq_ref[:, j:j + w2])
        qs = _dot(cq, wuqs_ref[:, j:j + w2])
        q_out[:, j:j + w2] = ((qh * ct + qs * st) * MLA_SOFTMAX_SCALE).astype(q_out.dtype)
        kb = (_dot(ckv, wk_ref[:, j:j + w2]) + roped2).astype(k_out.dtype)
        k_out[:, j:j + w2] = kb
        k2 = jnp.square(kb.astype(F32))
        for half in range(2):
            hh = j // LANES + half
            n = jnp.sum(jnp.where(shift_lane, 0.0, k2[:, half * LANES:(half + 1) * LANES]), axis=-1, keepdims=True)
            n = jnp.max(n, axis=0, keepdims=True)
            grp = hh // heads_per_group
            kn_rows[grp] = jnp.where(lane == hh % heads_per_group, n, kn_rows[grp])
    for grp, row in enumerate(kn_rows):
        kn_out[0, grp] = jnp.broadcast_to(row, kn_out.shape[2:])
    vt = lax.dot_general(wvt_ref[...], ckv, NT_DIMS, preferred_element_type=F32)
    ones_row = lax.broadcasted_iota(jnp.int32, (vt.shape[0], 1), 0) % MLA_VT_ROWS == MLA_V
    vt_out[0] = jnp.where(ones_row, 1.0, vt).astype(vt_out.dtype)


def _mla_prep(x2, s, g, wdq, qg, wuq, wuqs, wdkv, kvg, wk, wvt, ct, st, *, heads_per_group, tm=512):
    t, d = x2.shape
    nblk = s // tm
    full = lambda a: pl.BlockSpec(a.shape, lambda i: (0, 0))
    hw = MLA_HEADS * LANES
    groups = MLA_HEADS // heads_per_group
    return pl.pallas_call(
        functools.partial(_mla_prep_body, heads_per_group=heads_per_group),
        grid=(t // tm,),
        in_specs=[pl.BlockSpec((tm, d), lambda i: (i, 0)), full(g), full(wdq), full(qg), full(wuq), full(wuqs),
                  full(wdkv), full(kvg), full(wk), full(wvt),
                  pl.BlockSpec((tm, 2 * LANES), lambda i: (i % nblk, 0)),
                  pl.BlockSpec((tm, 2 * LANES), lambda i: (i % nblk, 0))],
        out_specs=[pl.BlockSpec((tm, hw), lambda i: (i, 0)),
                   pl.BlockSpec((tm, hw), lambda i: (i, 0)),
                   pl.BlockSpec((1, MLA_HEADS * MLA_VT_ROWS, tm), lambda i: (i // nblk, 0, i % nblk)),
                   pl.BlockSpec((1, groups, 8, LANES), lambda i: (i, 0, 0, 0))],
        out_shape=[jax.ShapeDtypeStruct((t, hw), BF16),
                   jax.ShapeDtypeStruct((t, hw), BF16),
                   jax.ShapeDtypeStruct((t // s, MLA_HEADS * MLA_VT_ROWS, s), BF16),
                   jax.ShapeDtypeStruct((t // tm, groups, 8, LANES), F32)],
        compiler_params=_cparams("parallel"),
        name="mla_prep",
    )(x2, g, wdq, qg, wuq, wuqs, wdkv, kvg, wk, wvt, ct, st)


def _mla_attn_body(q_ref, k_ref, vt_ref, kn_ref, o_ref, t_ref, *, heads):
    shift_lane = lax.broadcasted_iota(jnp.int32, (1, LANES), 1) == MLA_AUG_LANE
    kmax = jnp.sqrt(jnp.max(kn_ref[:, 0, 0:1, :], axis=0))

    def values(hh, et, halves):
        ot = _dot(vt_ref[0, hh * MLA_VT_ROWS:(hh + 1) * MLA_VT_ROWS, :], et)
        denom = ot[MLA_V:MLA_V + 1]
        halves.append(ot[:MLA_V] / denom)
        if hh % 2 == 1:
            p = hh // 2
            o_ref[0, :, p * LANES:(p + 1) * LANES] = jnp.concatenate(halves, axis=0).T.astype(o_ref.dtype)
            halves.clear()
        return denom

    lmin = None
    halves = []
    for hh in range(heads):
        q = q_ref[0, :, hh * LANES:(hh + 1) * LANES]
        k = k_ref[0, :, hh * LANES:(hh + 1) * LANES]
        qn = jnp.sqrt(jnp.sum(jnp.square(q.astype(F32)), axis=-1, keepdims=True))
        shift = (qn * (kmax[:, hh:hh + 1] * -BOUND_SLACK)).astype(BF16)
        tt = lax.dot_general(k, jnp.where(shift_lane, shift, q), NT_DIMS, preferred_element_type=F32)
        denom = values(hh, jnp.exp2(tt).astype(BF16), halves)
        lmin = denom if lmin is None else jnp.minimum(lmin, denom)

    @pl.when(jnp.logical_not(jnp.min(lmin) >= DENOM_MIN))
    def _():
        halves = []
        for hh in range(heads):
            q = q_ref[0, :, hh * LANES:(hh + 1) * LANES]
            k = k_ref[0, :, hh * LANES:(hh + 1) * LANES]
            t_ref[...] = lax.dot_general(k, q, NT_DIMS, preferred_element_type=F32)
            m = jnp.max(t_ref[...], axis=0, keepdims=True)
            values(hh, jnp.exp2(t_ref[...] - m).astype(BF16), halves)


def _mla_attention(q, k, vt, kn, *, heads, tq=512):
    b, s, _ = q.shape
    groups = MLA_HEADS // heads
    return pl.pallas_call(
        functools.partial(_mla_attn_body, heads=heads),
        grid=(b, groups, s // tq),
        in_specs=[pl.BlockSpec((1, tq, heads * LANES), lambda bi, g, qi: (bi, qi, g)),
                  pl.BlockSpec((1, s, heads * LANES), lambda bi, g, qi: (bi, 0, g)),
                  pl.BlockSpec((1, heads * MLA_VT_ROWS, s), lambda bi, g, qi: (bi, g, 0)),
                  pl.BlockSpec((kn.shape[0] // b, 1, 8, LANES), lambda bi, g, qi: (bi, g, 0, 0))],
        out_specs=pl.BlockSpec((1, tq, heads * MLA_V), lambda bi, g, qi: (bi, qi, g)),
        out_shape=jax.ShapeDtypeStruct((b, s, MLA_HEADS * MLA_V), BF16),
        scratch_shapes=[pltpu.VMEM((s, tq), F32)],
        compiler_params=_cparams("parallel", "parallel", "parallel"),
        name="mla_attn",
    )(q, k, vt, kn)


def _mla_weights(w_uq, w_dkv, w_ukv):
    f = MLA_NOPE + MLA_ROPE
    half = MLA_ROPE // 2
    r = w_uq.shape[0]
    uq = w_uq.reshape(r, MLA_HEADS, f)
    zq = jnp.zeros((r, MLA_HEADS, LANES - f), w_uq.dtype)
    wuq = jnp.concatenate([uq, zq], axis=-1).reshape(r, MLA_HEADS * LANES)
    wuqs = jnp.concatenate([jnp.zeros((r, MLA_HEADS, MLA_NOPE), w_uq.dtype),
                            uq[..., MLA_NOPE + half:], uq[..., MLA_NOPE:MLA_NOPE + half], zq],
                           axis=-1).reshape(r, MLA_HEADS * LANES)
    d = w_dkv.shape[0]
    kr = w_dkv[:, MLA_KV_RANK:]
    z64 = jnp.zeros((d, MLA_NOPE), w_dkv.dtype)
    z32 = jnp.zeros((d, LANES - f), w_dkv.dtype)
    wdkv = jnp.concatenate([w_dkv[:, :MLA_KV_RANK], z64, kr, z32,
                            z64, kr[:, half:], kr[:, :half], z32], axis=-1)
    c = w_ukv.shape[0]
    ukv = w_ukv.reshape(c, MLA_HEADS, MLA_NOPE + MLA_V)
    wk = jnp.concatenate([ukv[..., :MLA_NOPE], jnp.zeros((c, MLA_HEADS, LANES - MLA_NOPE), w_ukv.dtype)],
                         axis=-1).reshape(c, MLA_HEADS * LANES)
    wvt = jnp.concatenate([ukv[..., MLA_NOPE:].transpose(1, 2, 0),
                           jnp.zeros((MLA_HEADS, MLA_VT_ROWS - MLA_V, c), w_ukv.dtype)], axis=1)
    wvt = wvt.reshape(MLA_HEADS * MLA_VT_ROWS, c)
    return wuq.astype(BF16), wuqs.astype(BF16), wdkv.astype(BF16), wk.astype(BF16), wvt.astype(BF16)


def _rope_lane_tables(s):
    inv = ROPE_THETA ** (-jnp.arange(0, MLA_ROPE, 2, dtype=F32) / MLA_ROPE)
    ang = jnp.arange(s, dtype=F32)[:, None] * inv[None, :]
    cos, sin = jnp.cos(ang), jnp.sin(ang)
    one = jnp.ones((s, MLA_NOPE), F32)
    z64 = jnp.zeros((s, MLA_NOPE), F32)
    z32 = jnp.zeros((s, LANES - MLA_NOPE - MLA_ROPE), F32)
    ct = jnp.concatenate([one, cos, cos, z32], axis=-1)
    st = jnp.concatenate([z64, -sin, sin, z32], axis=-1)
    return jnp.tile(ct, (1, 2)), jnp.tile(st, (1, 2))


def kernel(x, mix_norm_e, w_in_e, diff_lq1, diff_lk1, diff_lq2, diff_lk2, diff_subln_g, na_rpb, w_out_e,
           mix_norm_o, w_dq, q_norm_g, w_uq, w_dkv, kv_norm_g, w_ukv, w_o_mla, ffn_norm_g, w_ffn_gate,
           w_ffn_val, ffn_conv_w, ffn_conv_b, w_ffn_down, final_norm_g):
    b, s, d = x.shape
    t = b * s
    depth = ffn_norm_g.shape[0]
    wg_all, wv_all, wd_all = w_ffn_gate.astype(BF16), w_ffn_val.astype(BF16), w_ffn_down.astype(BF16)
    for i in range(depth):
        j = i // 2
        x2 = x.reshape(t, d)
        if i % 2 == 0:
            lam_init = 0.8 - 0.6 * math.exp(-0.3 * i)
            wq = DIFF_HEADS * LANES
            col_scale = np.ones((1, w_in_e.shape[-1]), np.float32)
            col_scale[:, :wq] = SOFTMAX_SCALE
            col_scale[:, 3 * wq:3 * wq + NA_HEADS * HEAD_DIM] = SOFTMAX_SCALE
            proj = _norm_proj(x2, mix_norm_e[j], w_in_e[j].astype(BF16), jnp.asarray(col_scale)).reshape(b, s, -1)
            lvec = jnp.stack([diff_lq1[j], diff_lk1[j], diff_lq2[j], diff_lk2[j]]).astype(F32)
            a = _diff_attention(proj, lvec, diff_subln_g[j], lam_init=lam_init)
            nb = _na_attention(proj, na_rpb[j], col0=3 * DIFF_HEADS)
            acts, w_out = [a.reshape(t, -1), nb.reshape(t, -1)], w_out_e[j].astype(BF16)
        else:
            wuq, wuqs, wdkv, wk, wvt = _mla_weights(w_uq[j], w_dkv[j], w_ukv[j])
            ct, st = _rope_lane_tables(s)
            q, k, vt, kn = _mla_prep(x2, s, mix_norm_o[j].reshape(1, d), w_dq[j].astype(BF16),
                                     q_norm_g[j].reshape(1, -1), wuq, wuqs, wdkv, kv_norm_g[j].reshape(1, -1),
                                     wk, wvt, ct, st, heads_per_group=MLA_HEADS_PER_STEP)
            o = _mla_attention(q.reshape(b, s, -1), k.reshape(b, s, -1), vt, kn, heads=MLA_HEADS_PER_STEP)
            acts, w_out = [o.reshape(t, -1)], w_o_mla[j].astype(BF16)
        x = _mix_ffn(acts, w_out, x2, s, ffn_norm_g[i], wg_all, wv_all, ffn_conv_w[i], ffn_conv_b[i], wd_all,
                     final_norm_g, layer=i, final_norm=(i == depth - 1)).reshape(b, s, d)
    return x
```

```python
import functools
import math

import numpy as np
import jax
import jax.numpy as jnp
from jax import lax
from jax.experimental import pallas as pl
from jax.experimental.pallas import tpu as pltpu

F32 = jnp.float32
BF16 = jnp.bfloat16

EPS = 1e-6
NEG_INF = -1e30
GRID_W = 64
HEAD_DIM = 64
DIFF_HEADS = 4
NA_HEADS = 8
NA_WIN_H = 8
NA_WIN_W = 16
MLA_HEADS = 16
MLA_NOPE = 64
MLA_ROPE = 32
MLA_V = 64
MLA_KV_RANK = 256
ROPE_THETA = 10000.0

LANES = 128
VMEM_LIMIT = 56 * 1024 * 1024
NT_DIMS = (((1,), (1,)), ((), ()))
LOG2E = math.log2(math.e)
SQRT_HALF = np.float32(math.sqrt(0.5))
SOFTMAX_SCALE = HEAD_DIM ** -0.5 * LOG2E
MLA_SOFTMAX_SCALE = (MLA_NOPE + MLA_ROPE) ** -0.5 * LOG2E
BOUND_SLACK = 1.0 + 2.0 ** -6
DENOM_MIN = 2.0 ** -60
MLA_AUG_LANE = MLA_NOPE + MLA_ROPE
MLA_HEADS_PER_STEP = 16
MLA_VT_ROWS = 80


def _cparams(*sem):
    return pltpu.CompilerParams(dimension_semantics=sem, vmem_limit_bytes=VMEM_LIMIT)


def _rms(xf, g):
    ms = jnp.mean(xf * xf, axis=-1, keepdims=True)
    return xf * lax.rsqrt(ms + EPS) * g


def _dot(a, b):
    return jnp.dot(a, b, preferred_element_type=F32)


def _low_lanes():
    return lax.broadcasted_iota(jnp.int32, (1, LANES), 1) < HEAD_DIM


def _exp_weights(t):
    return jnp.exp2(t - jnp.max(t, axis=-1, keepdims=True)).astype(BF16)


def _fill_ones_aug(vaug_ref, v_ref, n_blocks):
    ones = jnp.ones((v_ref.shape[1], LANES), BF16)
    for p in range(n_blocks):
        vaug_ref[:, 2 * p * LANES:(2 * p + 1) * LANES] = v_ref[0, :, p * LANES:(p + 1) * LANES]
        vaug_ref[:, (2 * p + 1) * LANES:(2 * p + 2) * LANES] = ones


def _norm_proj_body(x_ref, g_ref, w_ref, cs_ref, o_ref, kn_out, *, n_chunk):
    hn = _rms(x_ref[...], g_ref[...]).astype(BF16)
    lo = _low_lanes()
    lane = lax.broadcasted_iota(jnp.int32, (1, LANES), 1)
    for j in range(0, o_ref.shape[1], n_chunk):
        ob = (_dot(hn, w_ref[:, j:j + n_chunk]) * cs_ref[:, j:j + n_chunk]).astype(o_ref.dtype)
        o_ref[:, j:j + n_chunk] = ob
        if j == n_chunk:
            k2 = jnp.square(ob.astype(F32))
            kn = jnp.zeros((1, LANES), F32)
            for h in range(DIFF_HEADS):
                kh = k2[:, h * LANES:(h + 1) * LANES]
                n1 = jnp.max(jnp.sum(jnp.where(lo, kh, 0.0), axis=-1, keepdims=True), axis=0, keepdims=True)
                n2 = jnp.max(jnp.sum(jnp.where(lo, 0.0, kh), axis=-1, keepdims=True), axis=0, keepdims=True)
                kn = jnp.where(lane == 2 * h, n1, jnp.where(lane == 2 * h + 1, n2, kn))
            kn_out[0] = jnp.broadcast_to(kn, kn_out.shape[1:])


def _norm_proj(x2, g, w, col_scale, *, tm=512):
    t, d = x2.shape
    n = w.shape[1]
    return pl.pallas_call(
        functools.partial(_norm_proj_body, n_chunk=DIFF_HEADS * LANES),
        grid=(t // tm,),
        in_specs=[pl.BlockSpec((tm, d), lambda i: (i, 0)),
                  pl.BlockSpec((1, d), lambda i: (0, 0)),
                  pl.BlockSpec((d, n), lambda i: (0, 0)),
                  pl.BlockSpec((1, n), lambda i: (0, 0))],
        out_specs=[pl.BlockSpec((tm, n), lambda i: (i, 0)),
                   pl.BlockSpec((1, 8, LANES), lambda i: (i, 0, 0))],
        out_shape=[jax.ShapeDtypeStruct((t, n), BF16),
                   jax.ShapeDtypeStruct((t // tm, 8, LANES), F32)],
        compiler_params=_cparams("parallel"),
        name="norm_proj",
    )(x2, g.reshape(1, d), w, col_scale)


def _diff_attn_body(lvec_ref, q_ref, k_ref, v_ref, kn_ref, g_ref, o_ref, vaug_ref, t_ref, *, tq, lam_init):
    qi = pl.program_id(1)
    lo = _low_lanes()

    @pl.when(qi == 0)
    def _():
        _fill_ones_aug(vaug_ref, v_ref, DIFF_HEADS)

    kmax = jnp.sqrt(jnp.max(kn_ref[:, 0:1, :], axis=0)) * BOUND_SLACK

    s_len = k_ref.shape[1]
    row = qi * tq + lax.broadcasted_iota(jnp.int32, (tq, s_len), 0)
    col = lax.broadcasted_iota(jnp.int32, (tq, s_len), 1)
    dist = jnp.abs(row - col).astype(F32)
    lv = lvec_ref[...]
    lam = (jnp.exp(jnp.sum(lv[0:1] * lv[1:2], axis=-1, keepdims=True))
           - jnp.exp(jnp.sum(lv[2:3] * lv[3:4], axis=-1, keepdims=True)) + lam_init)

    def maps(h):
        q = q_ref[0, :, h * LANES:(h + 1) * LANES]
        k = k_ref[0, :, h * LANES:(h + 1) * LANES]
        zero = jnp.zeros_like(q)
        bias = (-LOG2E * 2.0 ** (-8.0 * (h + 1) / DIFF_HEADS)) * dist
        s1 = lax.dot_general(jnp.where(lo, q, zero), k, NT_DIMS, preferred_element_type=F32)
        s2 = lax.dot_general(jnp.where(lo, zero, q), k, NT_DIMS, preferred_element_type=F32)
        return q, s1, s2, bias

    def finish(h, o1, o2):
        o = o1[:, :LANES] / o1[:, LANES:] - lam * (o2[:, :LANES] / o2[:, LANES:])
        o_ref[0, :, h * LANES:(h + 1) * LANES] = (_rms(o, g_ref[...]) * (1.0 - lam_init)).astype(o_ref.dtype)

    lmin = None
    for h in range(DIFF_HEADS):
        q, s1, s2, bias = maps(h)
        q2 = jnp.square(q.astype(F32))
        b1 = (jnp.sqrt(jnp.sum(jnp.where(lo, q2, 0.0), axis=-1, keepdims=True))
              * kmax[:, 2 * h:2 * h + 1])
        b2 = (jnp.sqrt(jnp.sum(jnp.where(lo, 0.0, q2), axis=-1, keepdims=True))
              * kmax[:, 2 * h + 1:2 * h + 2])
        vaug = vaug_ref[:, 2 * h * LANES:(2 * h + 2) * LANES]
        o1 = _dot(jnp.exp2(s1 + (bias - b1)).astype(BF16), vaug)
        o2 = _dot(jnp.exp2(s2 + (bias - b2)).astype(BF16), vaug)
        finish(h, o1, o2)
        l = jnp.minimum(o1[:, LANES:], o2[:, LANES:])
        lmin = l if lmin is None else jnp.minimum(lmin, l)

    @pl.when(jnp.logical_not(jnp.min(lmin) >= DENOM_MIN))
    def _():
        for h in range(DIFF_HEADS):
            _, s1, s2, bias = maps(h)
            t_ref[0] = s1 + bias
            t_ref[1] = s2 + bias
            vaug = vaug_ref[:, 2 * h * LANES:(2 * h + 2) * LANES]
            finish(h, _dot(_exp_weights(t_ref[0]), vaug), _dot(_exp_weights(t_ref[1]), vaug))


def _diff_attention(proj, kn, lvec, subln_g, *, lam_init, tq=512):
    b, s, _ = proj.shape
    w = DIFF_HEADS * LANES
    return pl.pallas_call(
        functools.partial(_diff_attn_body, tq=tq, lam_init=lam_init),
        grid=(b, s // tq),
        in_specs=[pl.BlockSpec((4, HEAD_DIM), lambda bi, qi: (0, 0)),
                  pl.BlockSpec((1, tq, w), lambda bi, qi: (bi, qi, 0)),
                  pl.BlockSpec((1, s, w), lambda bi, qi: (bi, 0, 1)),
                  pl.BlockSpec((1, s, w), lambda bi, qi: (bi, 0, 2)),
                  pl.BlockSpec((kn.shape[0] // b, 8, LANES), lambda bi, qi: (bi, 0, 0)),
                  pl.BlockSpec((1, LANES), lambda bi, qi: (0, 0))],
        out_specs=pl.BlockSpec((1, tq, w), lambda bi, qi: (bi, qi, 0)),
        out_shape=jax.ShapeDtypeStruct((b, s, w), BF16),
        scratch_shapes=[pltpu.VMEM((s, 2 * w), BF16), pltpu.VMEM((2, tq, s), F32)],
        compiler_params=_cparams("parallel", "arbitrary"),
        name="diff_attn",
    )(lvec, proj, proj, proj, kn, subln_g.reshape(1, LANES))


def _na_bias_table(rpb, kh):
    qc = np.arange(GRID_W)[:, None]
    kc = np.arange(GRID_W)[None, :]
    wstart = np.clip(qc - NA_WIN_W // 2, 0, GRID_W - NA_WIN_W)
    valid = (kc >= wstart) & (kc < wstart + NA_WIN_W)
    dc = kc - qc + NA_WIN_W - 1
    onehot = ((dc[None] == np.arange(2 * NA_WIN_W - 1)[:, None, None]) & valid[None]).astype(np.float32)
    toep = jnp.einsum("hrc,cqk->hrqk", rpb.astype(F32) * LOG2E, onehot, precision=lax.Precision.HIGHEST)
    toep = jnp.where(valid[None, None], toep, NEG_INF)
    per_d = [toep[:, NA_WIN_H - 1 - d:NA_WIN_H - 1 - d + kh].transpose(0, 2, 1, 3) for d in range(kh)]
    return jnp.stack(per_d, axis=1).reshape(rpb.shape[0], kh, GRID_W, kh * GRID_W)


def _na_body(q_ref, k_ref, v_ref, bias_ref, o_ref, vaug_ref, *, rows_n, kh):
    lo = _low_lanes()
    _fill_ones_aug(vaug_ref, v_ref, 1)
    for r in range(rows_n):
        rs = min(max(r - kh // 2, 0), rows_n - kh)
        q = q_ref[0, r * GRID_W:(r + 1) * GRID_W, :]
        kwin = k_ref[0, rs * GRID_W:(rs + kh) * GRID_W, :]
        vwin = vaug_ref[rs * GRID_W:(rs + kh) * GRID_W, :]
        zero = jnp.zeros_like(q)
        q2 = jnp.concatenate([jnp.where(lo, q, zero), jnp.where(lo, zero, q)], axis=0)
        bias = bias_ref[:, r - rs].reshape(2 * GRID_W, kh * GRID_W)
        t = lax.dot_general(q2, kwin, NT_DIMS, preferred_element_type=F32) + bias
        o2 = _dot(_exp_weights(t), vwin)
        o2 = o2[:, :LANES] / o2[:, LANES:]
        o_ref[0, r * GRID_W:(r + 1) * GRID_W, :] = jnp.where(lo, o2[:GRID_W], o2[GRID_W:]).astype(o_ref.dtype)


def _na_attention(proj, rpb, *, col0):
    b, s, _ = proj.shape
    rows_n = s // GRID_W
    kh = min(NA_WIN_H, rows_n)
    pairs = NA_HEADS // 2
    bias = _na_bias_table(rpb, kh)
    return pl.pallas_call(
        functools.partial(_na_body, rows_n=rows_n, kh=kh),
        grid=(b, pairs),
        in_specs=[pl.BlockSpec((1, s, LANES), lambda bi, p: (bi, 0, col0 + p)),
                  pl.BlockSpec((1, s, LANES), lambda bi, p: (bi, 0, col0 + pairs + p)),
                  pl.BlockSpec((1, s, LANES), lambda bi, p: (bi, 0, col0 + 2 * pairs + p)),
                  pl.BlockSpec((2, kh, GRID_W, kh * GRID_W), lambda bi, p: (p, 0, 0, 0))],
        out_specs=pl.BlockSpec((1, s, LANES), lambda bi, p: (bi, 0, p)),
        out_shape=jax.ShapeDtypeStruct((b, s, pairs * LANES), BF16),
        scratch_shapes=[pltpu.VMEM((s, 2 * LANES), BF16)],
        compiler_params=_cparams("parallel", "parallel"),
        name="na_attn",
    )(proj, proj, proj, bias)


HALO = 16


def _mix_ffn_body(*refs, n_act, tiles_per_seq, final_norm):
    acts = refs[:3 * n_act]
    wos = refs[3 * n_act:4 * n_act]
    x_ref, xp_ref, xn_ref, g_ref, wg_ref, wv_ref, cw_ref, cb_ref, wd_ref, fg_ref, o_ref = refs[4 * n_act:]
    pos = pl.program_id(0) % tiles_per_seq
    tm = x_ref.shape[0]
    xe = jnp.concatenate([xp_ref[...], x_ref[...], xn_ref[...]], axis=0)
    for k in range(n_act):
        a_ref, ap_ref, an_ref = acts[3 * k:3 * k + 3]
        xe = xe + _dot(jnp.concatenate([ap_ref[...], a_ref[...], an_ref[...]], axis=0), wos[k][...])
    row = lax.broadcasted_iota(jnp.int32, (tm + 2 * HALO, 1), 0)
    first_kept = jnp.where(pos == 0, HALO, 0)
    end_kept = jnp.where(pos == tiles_per_seq - 1, HALO + tm, tm + 2 * HALO)
    xe = jnp.where((row >= first_kept) & (row < end_kept), xe, 0.0)
    y = _rms(xe, g_ref[...])
    a_ext = _dot(y.astype(BF16), wg_ref[...])
    cw = cw_ref[...]
    a = (cw[0:1] * a_ext[HALO - 1:HALO - 1 + tm] + cw[1:2] * a_ext[HALO:HALO + tm]
         + cw[2:3] * a_ext[HALO + 1:HALO + 1 + tm] + cb_ref[...])
    gate = 0.5 * a * (1.0 + lax.erf(a * SQRT_HALF))
    val = _dot(y[HALO:HALO + tm].astype(BF16), wv_ref[...])
    out = xe[HALO:HALO + tm] + _dot((gate * val).astype(BF16), wd_ref[...])
    o_ref[...] = _rms(out, fg_ref[...]) if final_norm else out


def _mix_ffn(acts, w_out, x2, s, g, w_gate, w_val, conv_w, conv_b, w_down, final_g, *, layer, final_norm, tm=512):
    t, d = x2.shape
    dff = w_gate.shape[2]
    wa = acts[0].shape[1]
    assert all(a.shape[1] == wa for a in acts) and wa * len(acts) == w_out.shape[0]
    hb = tm // HALO
    last = t // HALO - 1

    def tiles(width):
        return [pl.BlockSpec((tm, width), lambda i: (i, 0)),
                pl.BlockSpec((HALO, width), lambda i: (jnp.maximum(i * hb - 1, 0), 0)),
                pl.BlockSpec((HALO, width), lambda i: (jnp.minimum((i + 1) * hb, last), 0))]

    def const(shape, row_block=0):
        return pl.BlockSpec(shape, lambda i: (row_block, 0), pipeline_mode=pl.Buffered(1))

    def slab(w):
        return pl.BlockSpec((None,) + w.shape[1:], lambda i: (layer, 0, 0), pipeline_mode=pl.Buffered(1))

    consts = (g.reshape(1, d), w_gate, w_val, conv_w, conv_b.reshape(1, dff), w_down, final_g.reshape(1, d))
    act_args = [a for act in acts for a in (act, act, act)]
    act_specs = [sp for _ in acts for sp in tiles(wa)]
    return pl.pallas_call(
        functools.partial(_mix_ffn_body, n_act=len(acts), tiles_per_seq=s // tm, final_norm=final_norm),
        grid=(t // tm,),
        in_specs=(act_specs + [const((wa, d), k) for k in range(len(acts))] + tiles(d)
                  + [slab(c) if c.ndim == 3 else const(c.shape) for c in consts]),
        out_specs=pl.BlockSpec((tm, d), lambda i: (i, 0)),
        out_shape=jax.ShapeDtypeStruct((t, d), F32),
        compiler_params=_cparams("parallel"),
        name="mix_ffn",
    )(*act_args, *([w_out] * len(acts)), x2, x2, x2, *consts)


def _mla_prep_body(x_ref, g_ref, wdq_ref, qg_ref, wuq_ref, wuqs_ref, wdkv_ref, kvg_ref, wk_ref, wvt_ref,
                   ct_ref, st_ref, q_out, k_out, vt_out, kn_out, *, heads_per_group):
    hn = _rms(x_ref[...], g_ref[...]).astype(BF16)
    cq = _rms(_dot(hn, wdq_ref[...]), qg_ref[...]).astype(BF16)
    kva = _dot(hn, wdkv_ref[...])
    ckv = _rms(kva[:, :MLA_KV_RANK], kvg_ref[...]).astype(BF16)
    ct = ct_ref[...]
    st = st_ref[...]
    kr = kva[:, MLA_KV_RANK:MLA_KV_RANK + LANES]
    kr_sw = kva[:, MLA_KV_RANK + LANES:]
    roped = kr * ct[:, :LANES] + kr_sw * st[:, :LANES]
    shift_lane = lax.broadcasted_iota(jnp.int32, (1, LANES), 1) == MLA_AUG_LANE
    roped = jnp.where(shift_lane, 1.0, roped)
    roped2 = jnp.concatenate([roped, roped], axis=1)
    w2 = 2 * LANES
    lane = lax.broadcasted_iota(jnp.int32, (1, LANES), 1)
    kn_rows = [jnp.zeros((1, LANES), F32) for _ in range(MLA_HEADS // heads_per_group)]
    for j in range(0, q_out.shape[1], w2):
        qh = _dot(cq, wuq_ref[:, j:j + w2])
        qs = _dot(cq, wuqs_ref[:, j:j + w2])
        q_out[:, j:j + w2] = ((qh * ct + qs * st) * MLA_SOFTMAX_SCALE).astype(q_out.dtype)
        kb = (_dot(ckv, wk_ref[:, j:j + w2]) + roped2).astype(k_out.dtype)
        k_out[:, j:j + w2] = kb
        k2 = jnp.square(kb.astype(F32))
        for half in range(2):
            hh = j // LANES + half
            n = jnp.sum(jnp.where(shift_lane, 0.0, k2[:, half * LANES:(half + 1) * LANES]), axis=-1, keepdims=True)
            n = jnp.max(n, axis=0, keepdims=True)
            grp = hh // heads_per_group
            kn_rows[grp] = jnp.where(lane == hh % heads_per_group, n, kn_rows[grp])
    for grp, row in enumerate(kn_rows):
        kn_out[0, grp] = jnp.broadcast_to(row, kn_out.shape[2:])
    vt = lax.dot_general(wvt_ref[...], ckv, NT_DIMS, preferred_element_type=F32)
    ones_row = lax.broadcasted_iota(jnp.int32, (vt.shape[0], 1), 0) % MLA_VT_ROWS == MLA_V
    vt_out[0] = jnp.where(ones_row, 1.0, vt).astype(vt_out.dtype)


def _mla_prep(x2, s, g, wdq, qg, wuq, wuqs, wdkv, kvg, wk, wvt, ct, st, *, heads_per_group, tm=512):
    t, d = x2.shape
    nblk = s // tm
    full = lambda a: pl.BlockSpec(a.shape, lambda i: (0, 0))
    hw = MLA_HEADS * LANES
    groups = MLA_HEADS // heads_per_group
    return pl.pallas_call(
        functools.partial(_mla_prep_body, heads_per_group=heads_per_group),
        grid=(t // tm,),
        in_specs=[pl.BlockSpec((tm, d), lambda i: (i, 0)), full(g), full(wdq), full(qg), full(wuq), full(wuqs),
                  full(wdkv), full(kvg), full(wk), full(wvt),
                  pl.BlockSpec((tm, 2 * LANES), lambda i: (i % nblk, 0)),
                  pl.BlockSpec((tm, 2 * LANES), lambda i: (i % nblk, 0))],
        out_specs=[pl.BlockSpec((tm, hw), lambda i: (i, 0)),
                   pl.BlockSpec((tm, hw), lambda i: (i, 0)),
                   pl.BlockSpec((1, MLA_HEADS * MLA_VT_ROWS, tm), lambda i: (i // nblk, 0, i % nblk)),
                   pl.BlockSpec((1, groups, 8, LANES), lambda i: (i, 0, 0, 0))],
        out_shape=[jax.ShapeDtypeStruct((t, hw), BF16),
                   jax.ShapeDtypeStruct((t, hw), BF16),
                   jax.ShapeDtypeStruct((t // s, MLA_HEADS * MLA_VT_ROWS, s), BF16),
                   jax.ShapeDtypeStruct((t // tm, groups, 8, LANES), F32)],
        compiler_params=_cparams("parallel"),
        name="mla_prep",
    )(x2, g, wdq, qg, wuq, wuqs, wdkv, kvg, wk, wvt, ct, st)


def _mla_attn_body(q_ref, k_ref, vt_ref, kn_ref, o_ref, t_ref, *, heads):
    shift_lane = lax.broadcasted_iota(jnp.int32, (1, LANES), 1) == MLA_AUG_LANE
    kmax = jnp.sqrt(jnp.max(kn_ref[:, 0, 0:1, :], axis=0))

    def values(hh, et, halves):
        ot = _dot(vt_ref[0, hh * MLA_VT_ROWS:(hh + 1) * MLA_VT_ROWS, :], et)
        denom = ot[MLA_V:MLA_V + 1]
        halves.append(ot[:MLA_V] / denom)
        if hh % 2 == 1:
            p = hh // 2
            o_ref[0, :, p * LANES:(p + 1) * LANES] = jnp.concatenate(halves, axis=0).T.astype(o_ref.dtype)
            halves.clear()
        return denom

    lmin = None
    halves = []
    for hh in range(heads):
        q = q_ref[0, :, hh * LANES:(hh + 1) * LANES]
        k = k_ref[0, :, hh * LANES:(hh + 1) * LANES]
        qn = jnp.sqrt(jnp.sum(jnp.square(q.astype(F32)), axis=-1, keepdims=True))
        shift = (qn * (kmax[:, hh:hh + 1] * -BOUND_SLACK)).astype(BF16)
        tt = lax.dot_general(k, jnp.where(shift_lane, shift, q), NT_DIMS, preferred_element_type=F32)
        denom = values(hh, jnp.exp2(tt).astype(BF16), halves)
        lmin = denom if lmin is None else jnp.minimum(lmin, denom)

    @pl.when(jnp.logical_not(jnp.min(lmin) >= DENOM_MIN))
    def _():
        halves = []
        for hh in range(heads):
            q = q_ref[0, :, hh * LANES:(hh + 1) * LANES]
            k = k_ref[0, :, hh * LANES:(hh + 1) * LANES]
            t_ref[...] = lax.dot_general(k, q, NT_DIMS, preferred_element_type=F32)
            m = jnp.max(t_ref[...], axis=0, keepdims=True)
            values(hh, jnp.exp2(t_ref[...] - m).astype(BF16), halves)


def _mla_attention(q, k, vt, kn, *, heads, tq=512):
    b, s, _ = q.shape
    groups = MLA_HEADS // heads
    return pl.pallas_call(
        functools.partial(_mla_attn_body, heads=heads),
        grid=(b, groups, s // tq),
        in_specs=[pl.BlockSpec((1, tq, heads * LANES), lambda bi, g, qi: (bi, qi, g)),
                  pl.BlockSpec((1, s, heads * LANES), lambda bi, g, qi: (bi, 0, g)),
                  pl.BlockSpec((1, heads * MLA_VT_ROWS, s), lambda bi, g, qi: (bi, g, 0)),
                  pl.BlockSpec((kn.shape[0] // b, 1, 8, LANES), lambda bi, g, qi: (bi, g, 0, 0))],
        out_specs=pl.BlockSpec((1, tq, heads * MLA_V), lambda bi, g, qi: (bi, qi, g)),
        out_shape=jax.ShapeDtypeStruct((b, s, MLA_HEADS * MLA_V), BF16),
        scratch_shapes=[pltpu.VMEM((s, tq), F32)],
        compiler_params=_cparams("parallel", "parallel", "parallel"),
        name="mla_attn",
    )(q, k, vt, kn)


def _mla_weights(w_uq, w_dkv, w_ukv):
    f = MLA_NOPE + MLA_ROPE
    half = MLA_ROPE // 2
    r = w_uq.shape[0]
    uq = w_uq.reshape(r, MLA_HEADS, f)
    zq = jnp.zeros((r, MLA_HEADS, LANES - f), w_uq.dtype)
    wuq = jnp.concatenate([uq, zq], axis=-1).reshape(r, MLA_HEADS * LANES)
    wuqs = jnp.concatenate([jnp.zeros((r, MLA_HEADS, MLA_NOPE), w_uq.dtype),
                            uq[..., MLA_NOPE + half:], uq[..., MLA_NOPE:MLA_NOPE + half], zq],
                           axis=-1).reshape(r, MLA_HEADS * LANES)
    d = w_dkv.shape[0]
    kr = w_dkv[:, MLA_KV_RANK:]
    z64 = jnp.zeros((d, MLA_NOPE), w_dkv.dtype)
    z32 = jnp.zeros((d, LANES - f), w_dkv.dtype)
    wdkv = jnp.concatenate([w_dkv[:, :MLA_KV_RANK], z64, kr, z32,
                            z64, kr[:, half:], kr[:, :half], z32], axis=-1)
    c = w_ukv.shape[0]
    ukv = w_ukv.reshape(c, MLA_HEADS, MLA_NOPE + MLA_V)
    wk = jnp.concatenate([ukv[..., :MLA_NOPE], jnp.zeros((c, MLA_HEADS, LANES - MLA_NOPE), w_ukv.dtype)],
                         axis=-1).reshape(c, MLA_HEADS * LANES)
    wvt = jnp.concatenate([ukv[..., MLA_NOPE:].transpose(1, 2, 0),
                           jnp.zeros((MLA_HEADS, MLA_VT_ROWS - MLA_V, c), w_ukv.dtype)], axis=1)
    wvt = wvt.reshape(MLA_HEADS * MLA_VT_ROWS, c)
    return wuq.astype(BF16), wuqs.astype(BF16), wdkv.astype(BF16), wk.astype(BF16), wvt.astype(BF16)


def _rope_lane_tables(s):
    inv = ROPE_THETA ** (-jnp.arange(0, MLA_ROPE, 2, dtype=F32) / MLA_ROPE)
    ang = jnp.arange(s, dtype=F32)[:, None] * inv[None, :]
    cos, sin = jnp.cos(ang), jnp.sin(ang)
    one = jnp.ones((s, MLA_NOPE), F32)
    z64 = jnp.zeros((s, MLA_NOPE), F32)
    z32 = jnp.zeros((s, LANES - MLA_NOPE - MLA_ROPE), F32)
    ct = jnp.concatenate([one, cos, cos, z32], axis=-1)
    st = jnp.concatenate([z64, -sin, sin, z32], axis=-1)
    return jnp.tile(ct, (1, 2)), jnp.tile(st, (1, 2))


def kernel(x, mix_norm_e, w_in_e, diff_lq1, diff_lk1, diff_lq2, diff_lk2, diff_subln_g, na_rpb, w_out_e,
           mix_norm_o, w_dq, q_norm_g, w_uq, w_dkv, kv_norm_g, w_ukv, w_o_mla, ffn_norm_g, w_ffn_gate,
           w_ffn_val, ffn_conv_w, ffn_conv_b, w_ffn_down, final_norm_g):
    b, s, d = x.shape
    t = b * s
    depth = ffn_norm_g.shape[0]
    wg_all, wv_all, wd_all = w_ffn_gate.astype(BF16), w_ffn_val.astype(BF16), w_ffn_down.astype(BF16)
    for i in range(depth):
        j = i // 2
        x2 = x.reshape(t, d)
        if i % 2 == 0:
            lam_init = 0.8 - 0.6 * math.exp(-0.3 * i)
            wq = DIFF_HEADS * LANES
            col_scale = np.ones((1, w_in_e.shape[-1]), np.float32)
            col_scale[:, :wq] = SOFTMAX_SCALE
            col_scale[:, 3 * wq:3 * wq + NA_HEADS * HEAD_DIM] = SOFTMAX_SCALE
            proj, kn_d = _norm_proj(x2, mix_norm_e[j], w_in_e[j].astype(BF16), jnp.asarray(col_scale))
            proj = proj.reshape(b, s, -1)
            lvec = jnp.stack([diff_lq1[j], diff_lk1[j], diff_lq2[j], diff_lk2[j]]).astype(F32)
            a = _diff_attention(proj, kn_d, lvec, diff_subln_g[j], lam_init=lam_init)
            nb = _na_attention(proj, na_rpb[j], col0=3 * DIFF_HEADS)
            acts, w_out = [a.reshape(t, -1), nb.reshape(t, -1)], w_out_e[j].astype(BF16)
        else:
            wuq, wuqs, wdkv, wk, wvt = _mla_weights(w_uq[j], w_dkv[j], w_ukv[j])
            ct, st = _rope_lane_tables(s)
            q, k, vt, kn = _mla_prep(x2, s, mix_norm_o[j].reshape(1, d), w_dq[j].astype(BF16),
                                     q_norm_g[j].reshape(1, -1), wuq, wuqs, wdkv, kv_norm_g[j].reshape(1, -1),
                                     wk, wvt, ct, st, heads_per_group=MLA_HEADS_PER_STEP)
            o = _mla_attention(q.reshape(b, s, -1), k.reshape(b, s, -1), vt, kn, heads=MLA_HEADS_PER_STEP)
            acts, w_out = [o.reshape(t, -1)], w_o_mla[j].astype(BF16)
        x = _mix_ffn(acts, w_out, x2, s, ffn_norm_g[i], wg_all, wv_all, ffn_conv_w[i], ffn_conv_b[i], wd_all,
                     final_norm_g, layer=i, final_norm=(i == depth - 1)).reshape(b, s, d)
    return x
```

```python
import functools
import math

import numpy as np
import jax
import jax.numpy as jnp
from jax import lax
from jax.experimental import pallas as pl
from jax.experimental.pallas import tpu as pltpu

F32 = jnp.float32
BF16 = jnp.bfloat16

EPS = 1e-6
NEG_INF = -1e30
GRID_W = 64
HEAD_DIM = 64
DIFF_HEADS = 4
NA_HEADS = 8
NA_WIN_H = 8
NA_WIN_W = 16
MLA_HEADS = 16
MLA_NOPE = 64
MLA_ROPE = 32
MLA_V = 64
MLA_KV_RANK = 256
ROPE_THETA = 10000.0

LANES = 128
VMEM_LIMIT = 56 * 1024 * 1024
NT_DIMS = (((1,), (1,)), ((), ()))
LOG2E = math.log2(math.e)
SQRT_HALF = np.float32(math.sqrt(0.5))
SOFTMAX_SCALE = HEAD_DIM ** -0.5 * LOG2E
MLA_SOFTMAX_SCALE = (MLA_NOPE + MLA_ROPE) ** -0.5 * LOG2E
BOUND_SLACK = 1.0 + 2.0 ** -6
DENOM_MIN = 2.0 ** -60
MLA_AUG_LANE = MLA_NOPE + MLA_ROPE
MLA_HEADS_PER_STEP = 8
MLA_VT_ROWS = 80


def _cparams(*sem):
    return pltpu.CompilerParams(dimension_semantics=sem, vmem_limit_bytes=VMEM_LIMIT)


def _rms(xf, g):
    ms = jnp.mean(xf * xf, axis=-1, keepdims=True)
    return xf * lax.rsqrt(ms + EPS) * g


def _dot(a, b):
    return jnp.dot(a, b, preferred_element_type=F32)


def _low_lanes():
    return lax.broadcasted_iota(jnp.int32, (1, LANES), 1) < HEAD_DIM


def _exp_weights(t):
    return jnp.exp2(t - jnp.max(t, axis=-1, keepdims=True)).astype(BF16)


def _fill_ones_aug(vaug_ref, v_ref, n_blocks):
    ones = jnp.ones((v_ref.shape[1], LANES), BF16)
    for p in range(n_blocks):
        vaug_ref[:, 2 * p * LANES:(2 * p + 1) * LANES] = v_ref[0, :, p * LANES:(p + 1) * LANES]
        vaug_ref[:, (2 * p + 1) * LANES:(2 * p + 2) * LANES] = ones


def _norm_proj_body(x_ref, g_ref, w_ref, cs_ref, o_ref, kn_out, *, n_chunk):
    hn = _rms(x_ref[...], g_ref[...]).astype(BF16)
    lo = _low_lanes()
    lane = lax.broadcasted_iota(jnp.int32, (1, LANES), 1)
    for j in range(0, o_ref.shape[1], n_chunk):
        ob = (_dot(hn, w_ref[:, j:j + n_chunk]) * cs_ref[:, j:j + n_chunk]).astype(o_ref.dtype)
        o_ref[:, j:j + n_chunk] = ob
        if j == n_chunk:
            k2 = jnp.square(ob.astype(F32))
            kn = jnp.zeros((1, LANES), F32)
            for h in range(DIFF_HEADS):
                kh = k2[:, h * LANES:(h + 1) * LANES]
                n1 = jnp.max(jnp.sum(jnp.where(lo, kh, 0.0), axis=-1, keepdims=True), axis=0, keepdims=True)
                n2 = jnp.max(jnp.sum(jnp.where(lo, 0.0, kh), axis=-1, keepdims=True), axis=0, keepdims=True)
                kn = jnp.where(lane == 2 * h, n1, jnp.where(lane == 2 * h + 1, n2, kn))
            kn_out[0] = jnp.broadcast_to(kn, kn_out.shape[1:])


def _norm_proj(x2, g, w, col_scale, *, tm=512):
    t, d = x2.shape
    n = w.shape[1]
    return pl.pallas_call(
        functools.partial(_norm_proj_body, n_chunk=DIFF_HEADS * LANES),
        grid=(t // tm,),
        in_specs=[pl.BlockSpec((tm, d), lambda i: (i, 0)),
                  pl.BlockSpec((1, d), lambda i: (0, 0)),
                  pl.BlockSpec((d, n), lambda i: (0, 0)),
                  pl.BlockSpec((1, n), lambda i: (0, 0))],
        out_specs=[pl.BlockSpec((tm, n), lambda i: (i, 0)),
                   pl.BlockSpec((1, 8, LANES), lambda i: (i, 0, 0))],
        out_shape=[jax.ShapeDtypeStruct((t, n), BF16),
                   jax.ShapeDtypeStruct((t // tm, 8, LANES), F32)],
        compiler_params=_cparams("parallel"),
        name="norm_proj",
    )(x2, g.reshape(1, d), w, col_scale)


def _diff_attn_body(lvec_ref, q_ref, k_ref, v_ref, kn_ref, g_ref, o_ref, vaug_ref, t_ref, *, tq, lam_init):
    qi = pl.program_id(1)
    lo = _low_lanes()

    @pl.when(qi == 0)
    def _():
        _fill_ones_aug(vaug_ref, v_ref, DIFF_HEADS)

    kmax = jnp.sqrt(jnp.max(kn_ref[:, 0:1, :], axis=0)) * BOUND_SLACK

    s_len = k_ref.shape[1]
    row = qi * tq + lax.broadcasted_iota(jnp.int32, (tq, s_len), 0)
    col = lax.broadcasted_iota(jnp.int32, (tq, s_len), 1)
    dist = jnp.abs(row - col).astype(F32)
    lv = lvec_ref[...]
    lam = (jnp.exp(jnp.sum(lv[0:1] * lv[1:2], axis=-1, keepdims=True))
           - jnp.exp(jnp.sum(lv[2:3] * lv[3:4], axis=-1, keepdims=True)) + lam_init)

    def maps(h):
        q = q_ref[0, :, h * LANES:(h + 1) * LANES]
        k = k_ref[0, :, h * LANES:(h + 1) * LANES]
        zero = jnp.zeros_like(q)
        bias = (-LOG2E * 2.0 ** (-8.0 * (h + 1) / DIFF_HEADS)) * dist
        s1 = lax.dot_general(jnp.where(lo, q, zero), k, NT_DIMS, preferred_element_type=F32)
        s2 = lax.dot_general(jnp.where(lo, zero, q), k, NT_DIMS, preferred_element_type=F32)
        return q, s1, s2, bias

    def finish(h, o1, o2):
        o = o1[:, :LANES] / o1[:, LANES:] - lam * (o2[:, :LANES] / o2[:, LANES:])
        o_ref[0, :, h * LANES:(h + 1) * LANES] = (_rms(o, g_ref[...]) * (1.0 - lam_init)).astype(o_ref.dtype)

    lmin = None
    for h in range(DIFF_HEADS):
        q, s1, s2, bias = maps(h)
        q2 = jnp.square(q.astype(F32))
        b1 = (jnp.sqrt(jnp.sum(jnp.where(lo, q2, 0.0), axis=-1, keepdims=True))
              * kmax[:, 2 * h:2 * h + 1])
        b2 = (jnp.sqrt(jnp.sum(jnp.where(lo, 0.0, q2), axis=-1, keepdims=True))
              * kmax[:, 2 * h + 1:2 * h + 2])
        vaug = vaug_ref[:, 2 * h * LANES:(2 * h + 2) * LANES]
        o1 = _dot(jnp.exp2(s1 + (bias - b1)).astype(BF16), vaug)
        o2 = _dot(jnp.exp2(s2 + (bias - b2)).astype(BF16), vaug)
        finish(h, o1, o2)
        l = jnp.minimum(o1[:, LANES:], o2[:, LANES:])
        lmin = l if lmin is None else jnp.minimum(lmin, l)

    @pl.when(jnp.logical_not(jnp.min(lmin) >= DENOM_MIN))
    def _():
        for h in range(DIFF_HEADS):
            _, s1, s2, bias = maps(h)
            t_ref[0] = s1 + bias
            t_ref[1] = s2 + bias
            vaug = vaug_ref[:, 2 * h * LANES:(2 * h + 2) * LANES]
            finish(h, _dot(_exp_weights(t_ref[0]), vaug), _dot(_exp_weights(t_ref[1]), vaug))


def _diff_attention(proj, kn, lvec, subln_g, *, lam_init, tq=512):
    b, s, _ = proj.shape
    w = DIFF_HEADS * LANES
    return pl.pallas_call(
        functools.partial(_diff_attn_body, tq=tq, lam_init=lam_init),
        grid=(b, s // tq),
        in_specs=[pl.BlockSpec((4, HEAD_DIM), lambda bi, qi: (0, 0)),
                  pl.BlockSpec((1, tq, w), lambda bi, qi: (bi, qi, 0)),
                  pl.BlockSpec((1, s, w), lambda bi, qi: (bi, 0, 1)),
                  pl.BlockSpec((1, s, w), lambda bi, qi: (bi, 0, 2)),
                  pl.BlockSpec((kn.shape[0] // b, 8, LANES), lambda bi, qi: (bi, 0, 0)),
                  pl.BlockSpec((1, LANES), lambda bi, qi: (0, 0))],
        out_specs=pl.BlockSpec((1, tq, w), lambda bi, qi: (bi, qi, 0)),
        out_shape=jax.ShapeDtypeStruct((b, s, w), BF16),
        scratch_shapes=[pltpu.VMEM((s, 2 * w), BF16), pltpu.VMEM((2, tq, s), F32)],
        compiler_params=_cparams("parallel", "arbitrary"),
        name="diff_attn",
    )(lvec, proj, proj, proj, kn, subln_g.reshape(1, LANES))


def _na_bias_table(rpb, kh):
    qc = np.arange(GRID_W)[:, None]
    kc = np.arange(GRID_W)[None, :]
    wstart = np.clip(qc - NA_WIN_W // 2, 0, GRID_W - NA_WIN_W)
    valid = (kc >= wstart) & (kc < wstart + NA_WIN_W)
    dc = kc - qc + NA_WIN_W - 1
    onehot = ((dc[None] == np.arange(2 * NA_WIN_W - 1)[:, None, None]) & valid[None]).astype(np.float32)
    toep = jnp.einsum("hrc,cqk->hrqk", rpb.astype(F32) * LOG2E, onehot, precision=lax.Precision.HIGHEST)
    toep = jnp.where(valid[None, None], toep, NEG_INF)
    per_d = [toep[:, NA_WIN_H - 1 - d:NA_WIN_H - 1 - d + kh].transpose(0, 2, 1, 3) for d in range(kh)]
    return jnp.stack(per_d, axis=1).reshape(rpb.shape[0], kh, GRID_W, kh * GRID_W)


def _na_body(q_ref, k_ref, v_ref, bias_ref, o_ref, vaug_ref, *, rows_n, kh):
    lo = _low_lanes()
    _fill_ones_aug(vaug_ref, v_ref, 1)
    for r in range(rows_n):
        rs = min(max(r - kh // 2, 0), rows_n - kh)
        q = q_ref[0, r * GRID_W:(r + 1) * GRID_W, :]
        kwin = k_ref[0, rs * GRID_W:(rs + kh) * GRID_W, :]
        vwin = vaug_ref[rs * GRID_W:(rs + kh) * GRID_W, :]
        zero = jnp.zeros_like(q)
        q2 = jnp.concatenate([jnp.where(lo, q, zero), jnp.where(lo, zero, q)], axis=0)
        bias = bias_ref[:, r - rs].reshape(2 * GRID_W, kh * GRID_W)
        t = lax.dot_general(q2, kwin, NT_DIMS, preferred_element_type=F32) + bias
        o2 = _dot(_exp_weights(t), vwin)
        o2 = o2[:, :LANES] / o2[:, LANES:]
        o_ref[0, r * GRID_W:(r + 1) * GRID_W, :] = jnp.where(lo, o2[:GRID_W], o2[GRID_W:]).astype(o_ref.dtype)


def _na_attention(proj, rpb, *, col0):
    b, s, _ = proj.shape
    rows_n = s // GRID_W
    kh = min(NA_WIN_H, rows_n)
    pairs = NA_HEADS // 2
    bias = _na_bias_table(rpb, kh)
    return pl.pallas_call(
        functools.partial(_na_body, rows_n=rows_n, kh=kh),
        grid=(b, pairs),
        in_specs=[pl.BlockSpec((1, s, LANES), lambda bi, p: (bi, 0, col0 + p)),
                  pl.BlockSpec((1, s, LANES), lambda bi, p: (bi, 0, col0 + pairs + p)),
                  pl.BlockSpec((1, s, LANES), lambda bi, p: (bi, 0, col0 + 2 * pairs + p)),
                  pl.BlockSpec((2, kh, GRID_W, kh * GRID_W), lambda bi, p: (p, 0, 0, 0))],
        out_specs=pl.BlockSpec((1, s, LANES), lambda bi, p: (bi, 0, p)),
        out_shape=jax.ShapeDtypeStruct((b, s, pairs * LANES), BF16),
        scratch_shapes=[pltpu.VMEM((s, 2 * LANES), BF16)],
        compiler_params=_cparams("parallel", "parallel"),
        name="na_attn",
    )(proj, proj, proj, bias)


HALO = 16


def _mix_ffn_body(*refs, n_act, tiles_per_seq, final_norm):
    acts = refs[:3 * n_act]
    wos = refs[3 * n_act:4 * n_act]
    x_ref, xp_ref, xn_ref, g_ref, wg_ref, wv_ref, cw_ref, cb_ref, wd_ref, fg_ref, o_ref = refs[4 * n_act:]
    pos = pl.program_id(0) % tiles_per_seq
    tm = x_ref.shape[0]
    xe = jnp.concatenate([xp_ref[...], x_ref[...], xn_ref[...]], axis=0)
    for k in range(n_act):
        a_ref, ap_ref, an_ref = acts[3 * k:3 * k + 3]
        xe = xe + _dot(jnp.concatenate([ap_ref[...], a_ref[...], an_ref[...]], axis=0), wos[k][...])
    row = lax.broadcasted_iota(jnp.int32, (tm + 2 * HALO, 1), 0)
    first_kept = jnp.where(pos == 0, HALO, 0)
    end_kept = jnp.where(pos == tiles_per_seq - 1, HALO + tm, tm + 2 * HALO)
    xe = jnp.where((row >= first_kept) & (row < end_kept), xe, 0.0)
    y = _rms(xe, g_ref[...])
    a_ext = _dot(y.astype(BF16), wg_ref[...])
    cw = cw_ref[...]
    a = (cw[0:1] * a_ext[HALO - 1:HALO - 1 + tm] + cw[1:2] * a_ext[HALO:HALO + tm]
         + cw[2:3] * a_ext[HALO + 1:HALO + 1 + tm] + cb_ref[...])
    gate = 0.5 * a * (1.0 + lax.erf(a * SQRT_HALF))
    val = _dot(y[HALO:HALO + tm].astype(BF16), wv_ref[...])
    out = xe[HALO:HALO + tm] + _dot((gate * val).astype(BF16), wd_ref[...])
    o_ref[...] = _rms(out, fg_ref[...]) if final_norm else out


def _mix_ffn(acts, w_out, x2, s, g, w_gate, w_val, conv_w, conv_b, w_down, final_g, *, layer, final_norm, tm=512):
    t, d = x2.shape
    dff = w_gate.shape[2]
    wa = acts[0].shape[1]
    assert all(a.shape[1] == wa for a in acts) and wa * len(acts) == w_out.shape[0]
    hb = tm // HALO
    last = t // HALO - 1

    def tiles(width):
        return [pl.BlockSpec((tm, width), lambda i: (i, 0)),
                pl.BlockSpec((HALO, width), lambda i: (jnp.maximum(i * hb - 1, 0), 0)),
                pl.BlockSpec((HALO, width), lambda i: (jnp.minimum((i + 1) * hb, last), 0))]

    def const(shape, row_block=0):
        return pl.BlockSpec(shape, lambda i: (row_block, 0), pipeline_mode=pl.Buffered(1))

    def slab(w):
        return pl.BlockSpec((None,) + w.shape[1:], lambda i: (layer, 0, 0), pipeline_mode=pl.Buffered(1))

    consts = (g.reshape(1, d), w_gate, w_val, conv_w, conv_b.reshape(1, dff), w_down, final_g.reshape(1, d))
    act_args = [a for act in acts for a in (act, act, act)]
    act_specs = [sp for _ in acts for sp in tiles(wa)]
    return pl.pallas_call(
        functools.partial(_mix_ffn_body, n_act=len(acts), tiles_per_seq=s // tm, final_norm=final_norm),
        grid=(t // tm,),
        in_specs=(act_specs + [const((wa, d), k) for k in range(len(acts))] + tiles(d)
                  + [slab(c) if c.ndim == 3 else const(c.shape) for c in consts]),
        out_specs=pl.BlockSpec((tm, d), lambda i: (i, 0)),
        out_shape=jax.ShapeDtypeStruct((t, d), F32),
        compiler_params=_cparams("parallel"),
        name="mix_ffn",
    )(*act_args, *([w_out] * len(acts)), x2, x2, x2, *consts)


def _mla_prep_body(x_ref, g_ref, wdq_ref, qg_ref, wuq_ref, wuqs_ref, wdkv_ref, kvg_ref, wk_ref, wvt_ref,
                   ct_ref, st_ref, q_out, k_out, vt_out, kn_out, *, heads_per_group):
    hn = _rms(x_ref[...], g_ref[...]).astype(BF16)
    cq = _rms(_dot(hn, wdq_ref[...]), qg_ref[...]).astype(BF16)
    kva = _dot(hn, wdkv_ref[...])
    ckv = _rms(kva[:, :MLA_KV_RANK], kvg_ref[...]).astype(BF16)
    ct = ct_ref[...]
    st = st_ref[...]
    kr = kva[:, MLA_KV_RANK:MLA_KV_RANK + LANES]
    kr_sw = kva[:, MLA_KV_RANK + LANES:]
    roped = kr * ct[:, :LANES] + kr_sw * st[:, :LANES]
    shift_lane = lax.broadcasted_iota(jnp.int32, (1, LANES), 1) == MLA_AUG_LANE
    roped = jnp.where(shift_lane, 1.0, roped)
    roped2 = jnp.concatenate([roped, roped], axis=1)
    w2 = 2 * LANES
    lane = lax.broadcasted_iota(jnp.int32, (1, LANES), 1)
    kn_rows = [jnp.zeros((1, LANES), F32) for _ in range(MLA_HEADS // heads_per_group)]
    for j in range(0, q_out.shape[1], w2):
        qh = _dot(cq, wuq_ref[:, j:j + w2])
        qs = _dot(cq, wuqs_ref[:, j:j + w2])
        q_out[:, j:j + w2] = ((qh * ct + qs * st) * MLA_SOFTMAX_SCALE).astype(q_out.dtype)
        kb = (_dot(ckv, wk_ref[:, j:j + w2]) + roped2).astype(k_out.dtype)
        k_out[:, j:j + w2] = kb
        k2 = jnp.square(kb.astype(F32))
        for half in range(2):
            hh = j // LANES + half
            n = jnp.sum(jnp.where(shift_lane, 0.0, k2[:, half * LANES:(half + 1) * LANES]), axis=-1, keepdims=True)
            n = jnp.max(n, axis=0, keepdims=True)
            grp = hh // heads_per_group
            kn_rows[grp] = jnp.where(lane == hh % heads_per_group, n, kn_rows[grp])
    for grp, row in enumerate(kn_rows):
        kn_out[0, grp] = jnp.broadcast_to(row, kn_out.shape[2:])
    vt = lax.dot_general(wvt_ref[...], ckv, NT_DIMS, preferred_element_type=F32)
    ones_row = lax.broadcasted_iota(jnp.int32, (vt.shape[0], 1), 0) % MLA_VT_ROWS == MLA_V
    vt_out[0] = jnp.where(ones_row, 1.0, vt).astype(vt_out.dtype)


def _mla_prep(x2, s, g, wdq, qg, wuq, wuqs, wdkv, kvg, wk, wvt, ct, st, *, heads_per_group, tm=512):
    t, d = x2.shape
    nblk = s // tm
    full = lambda a: pl.BlockSpec(a.shape, lambda i: (0, 0))
    hw = MLA_HEADS * LANES
    groups = MLA_HEADS // heads_per_group
    return pl.pallas_call(
        functools.partial(_mla_prep_body, heads_per_group=heads_per_group),
        grid=(t // tm,),
        in_specs=[pl.BlockSpec((tm, d), lambda i: (i, 0)), full(g), full(wdq), full(qg), full(wuq), full(wuqs),
                  full(wdkv), full(kvg), full(wk), full(wvt),
                  pl.BlockSpec((tm, 2 * LANES), lambda i: (i % nblk, 0)),
                  pl.BlockSpec((tm, 2 * LANES), lambda i: (i % nblk, 0))],
        out_specs=[pl.BlockSpec((tm, hw), lambda i: (i, 0)),
                   pl.BlockSpec((tm, hw), lambda i: (i, 0)),
                   pl.BlockSpec((1, MLA_HEADS * MLA_VT_ROWS, tm), lambda i: (i // nblk, 0, i % nblk)),
                   pl.BlockSpec((1, groups, 8, LANES), lambda i: (i, 0, 0, 0))],
        out_shape=[jax.ShapeDtypeStruct((t, hw), BF16),
                   jax.ShapeDtypeStruct((t, hw), BF16),
                   jax.ShapeDtypeStruct((t // s, MLA_HEADS * MLA_VT_ROWS, s), BF16),
                   jax.ShapeDtypeStruct((t // tm, groups, 8, LANES), F32)],
        compiler_params=_cparams("parallel"),
        name="mla_prep",
    )(x2, g, wdq, qg, wuq, wuqs, wdkv, kvg, wk, wvt, ct, st)


def _mla_attn_body(q_ref, k_ref, vt_ref, kn_ref, o_ref, t_ref, *, heads):
    shift_lane = lax.broadcasted_iota(jnp.int32, (1, LANES), 1) == MLA_AUG_LANE
    kmax = jnp.sqrt(jnp.max(kn_ref[:, 0, 0:1, :], axis=0))

    def values(hh, et, halves):
        ot = _dot(vt_ref[0, hh * MLA_VT_ROWS:(hh + 1) * MLA_VT_ROWS, :], et)
        denom = ot[MLA_V:MLA_V + 1]
        halves.append(ot[:MLA_V] / denom)
        if hh % 2 == 1:
            p = hh // 2
            o_ref[0, :, p * LANES:(p + 1) * LANES] = jnp.concatenate(halves, axis=0).T.astype(o_ref.dtype)
            halves.clear()
        return denom

    lmin = None
    halves = []
    for hh in range(heads):
        q = q_ref[0, :, hh * LANES:(hh + 1) * LANES]
        k = k_ref[0, :, hh * LANES:(hh + 1) * LANES]
        qn = jnp.sqrt(jnp.sum(jnp.square(q.astype(F32)), axis=-1, keepdims=True))
        shift = (qn * (kmax[:, hh:hh + 1] * -BOUND_SLACK)).astype(BF16)
        tt = lax.dot_general(k, jnp.where(shift_lane, shift, q), NT_DIMS, preferred_element_type=F32)
        denom = values(hh, jnp.exp2(tt).astype(BF16), halves)
        lmin = denom if lmin is None else jnp.minimum(lmin, denom)

    @pl.when(jnp.logical_not(jnp.min(lmin) >= DENOM_MIN))
    def _():
        halves = []
        for hh in range(heads):
            q = q_ref[0, :, hh * LANES:(hh + 1) * LANES]
            k = k_ref[0, :, hh * LANES:(hh + 1) * LANES]
            t_ref[...] = lax.dot_general(k, q, NT_DIMS, preferred_element_type=F32)
            m = jnp.max(t_ref[...], axis=0, keepdims=True)
            values(hh, jnp.exp2(t_ref[...] - m).astype(BF16), halves)


def _mla_attention(q, k, vt, kn, *, heads, tq=512):
    b, s, _ = q.shape
    groups = MLA_HEADS // heads
    return pl.pallas_call(
        functools.partial(_mla_attn_body, heads=heads),
        grid=(b, groups, s // tq),
        in_specs=[pl.BlockSpec((1, tq, heads * LANES), lambda bi, g, qi: (bi, qi, g)),
                  pl.BlockSpec((1, s, heads * LANES), lambda bi, g, qi: (bi, 0, g)),
                  pl.BlockSpec((1, heads * MLA_VT_ROWS, s), lambda bi, g, qi: (bi, g, 0)),
                  pl.BlockSpec((kn.shape[0] // b, 1, 8, LANES), lambda bi, g, qi: (bi, g, 0, 0))],
        out_specs=pl.BlockSpec((1, tq, heads * MLA_V), lambda bi, g, qi: (bi, qi, g)),
        out_shape=jax.ShapeDtypeStruct((b, s, MLA_HEADS * MLA_V), BF16),
        scratch_shapes=[pltpu.VMEM((s, tq), F32)],
        compiler_params=_cparams("parallel", "parallel", "parallel"),
        name="mla_attn",
    )(q, k, vt, kn)


def _mla_weights(w_uq, w_dkv, w_ukv):
    f = MLA_NOPE + MLA_ROPE
    half = MLA_ROPE // 2
    r = w_uq.shape[0]
    uq = w_uq.reshape(r, MLA_HEADS, f)
    zq = jnp.zeros((r, MLA_HEADS, LANES - f), w_uq.dtype)
    wuq = jnp.concatenate([uq, zq], axis=-1).reshape(r, MLA_HEADS * LANES)
    wuqs = jnp.concatenate([jnp.zeros((r, MLA_HEADS, MLA_NOPE), w_uq.dtype),
                            uq[..., MLA_NOPE + half:], uq[..., MLA_NOPE:MLA_NOPE + half], zq],
                           axis=-1).reshape(r, MLA_HEADS * LANES)
    d = w_dkv.shape[0]
    kr = w_dkv[:, MLA_KV_RANK:]
    z64 = jnp.zeros((d, MLA_NOPE), w_dkv.dtype)
    z32 = jnp.zeros((d, LANES - f), w_dkv.dtype)
    wdkv = jnp.concatenate([w_dkv[:, :MLA_KV_RANK], z64, kr, z32,
                            z64, kr[:, half:], kr[:, :half], z32], axis=-1)
    c = w_ukv.shape[0]
    ukv = w_ukv.reshape(c, MLA_HEADS, MLA_NOPE + MLA_V)
    wk = jnp.concatenate([ukv[..., :MLA_NOPE], jnp.zeros((c, MLA_HEADS, LANES - MLA_NOPE), w_ukv.dtype)],
                         axis=-1).reshape(c, MLA_HEADS * LANES)
    wvt = jnp.concatenate([ukv[..., MLA_NOPE:].transpose(1, 2, 0),
                           jnp.zeros((MLA_HEADS, MLA_VT_ROWS - MLA_V, c), w_ukv.dtype)], axis=1)
    wvt = wvt.reshape(MLA_HEADS * MLA_VT_ROWS, c)
    return wuq.astype(BF16), wuqs.astype(BF16), wdkv.astype(BF16), wk.astype(BF16), wvt.astype(BF16)


def _rope_lane_tables(s):
    inv = ROPE_THETA ** (-jnp.arange(0, MLA_ROPE, 2, dtype=F32) / MLA_ROPE)
    ang = jnp.arange(s, dtype=F32)[:, None] * inv[None, :]
    cos, sin = jnp.cos(ang), jnp.sin(ang)
    one = jnp.ones((s, MLA_NOPE), F32)
    z64 = jnp.zeros((s, MLA_NOPE), F32)
    z32 = jnp.zeros((s, LANES - MLA_NOPE - MLA_ROPE), F32)
    ct = jnp.concatenate([one, cos, cos, z32], axis=-1)
    st = jnp.concatenate([z64, -sin, sin, z32], axis=-1)
    return jnp.tile(ct, (1, 2)), jnp.tile(st, (1, 2))


def kernel(x, mix_norm_e, w_in_e, diff_lq1, diff_lk1, diff_lq2, diff_lk2, diff_subln_g, na_rpb, w_out_e,
           mix_norm_o, w_dq, q_norm_g, w_uq, w_dkv, kv_norm_g, w_ukv, w_o_mla, ffn_norm_g, w_ffn_gate,
           w_ffn_val, ffn_conv_w, ffn_conv_b, w_ffn_down, final_norm_g):
    b, s, d = x.shape
    t = b * s
    depth = ffn_norm_g.shape[0]
    wg_all, wv_all, wd_all = w_ffn_gate.astype(BF16), w_ffn_val.astype(BF16), w_ffn_down.astype(BF16)
    for i in range(depth):
        j = i // 2
        x2 = x.reshape(t, d)
        if i % 2 == 0:
            lam_init = 0.8 - 0.6 * math.exp(-0.3 * i)
            wq = DIFF_HEADS * LANES
            col_scale = np.ones((1, w_in_e.shape[-1]), np.float32)
            col_scale[:, :wq] = SOFTMAX_SCALE
            col_scale[:, 3 * wq:3 * wq + NA_HEADS * HEAD_DIM] = SOFTMAX_SCALE
            proj, kn_d = _norm_proj(x2, mix_norm_e[j], w_in_e[j].astype(BF16), jnp.asarray(col_scale))
            proj = proj.reshape(b, s, -1)
            lvec = jnp.stack([diff_lq1[j], diff_lk1[j], diff_lq2[j], diff_lk2[j]]).astype(F32)
            a = _diff_attention(proj, kn_d, lvec, diff_subln_g[j], lam_init=lam_init)
            nb = _na_attention(proj, na_rpb[j], col0=3 * DIFF_HEADS)
            acts, w_out = [a.reshape(t, -1), nb.reshape(t, -1)], w_out_e[j].astype(BF16)
        else:
            wuq, wuqs, wdkv, wk, wvt = _mla_weights(w_uq[j], w_dkv[j], w_ukv[j])
            ct, st = _rope_lane_tables(s)
            q, k, vt, kn = _mla_prep(x2, s, mix_norm_o[j].reshape(1, d), w_dq[j].astype(BF16),
                                     q_norm_g[j].reshape(1, -1), wuq, wuqs, wdkv, kv_norm_g[j].reshape(1, -1),
                                     wk, wvt, ct, st, heads_per_group=MLA_HEADS_PER_STEP)
            o = _mla_attention(q.reshape(b, s, -1), k.reshape(b, s, -1), vt, kn, heads=MLA_HEADS_PER_STEP)
            acts, w_out = [o.reshape(t, -1)], w_o_mla[j].astype(BF16)
        x = _mix_ffn(acts, w_out, x2, s, ffn_norm_g[i], wg_all, wv_all, ffn_conv_w[i], ffn_conv_b[i], wd_all,
                     final_norm_g, layer=i, final_norm=(i == depth - 1)).reshape(b, s, d)
    return x
```

```python
import functools
import math

import numpy as np
import jax
import jax.numpy as jnp
from jax import lax
from jax.experimental import pallas as pl
from jax.experimental.pallas import tpu as pltpu

F32 = jnp.float32
BF16 = jnp.bfloat16

EPS = 1e-6
NEG_INF = -1e30
GRID_W = 64
HEAD_DIM = 64
DIFF_HEADS = 4
NA_HEADS = 8
NA_WIN_H = 8
NA_WIN_W = 16
MLA_HEADS = 16
MLA_NOPE = 64
MLA_ROPE = 32
MLA_V = 64
MLA_KV_RANK = 256
ROPE_THETA = 10000.0

LANES = 128
VMEM_LIMIT = 56 * 1024 * 1024
NT_DIMS = (((1,), (1,)), ((), ()))
LOG2E = math.log2(math.e)
SQRT_HALF = np.float32(math.sqrt(0.5))
SOFTMAX_SCALE = HEAD_DIM ** -0.5 * LOG2E
MLA_SOFTMAX_SCALE = (MLA_NOPE + MLA_ROPE) ** -0.5 * LOG2E
BOUND_SLACK = 1.0 + 2.0 ** -6
DENOM_MIN = 2.0 ** -60
MLA_AUG_LANE = MLA_NOPE + MLA_ROPE
MLA_HEADS_PER_STEP = 8
MLA_VT_ROWS = 80


def _cparams(*sem):
    return pltpu.CompilerParams(dimension_semantics=sem, vmem_limit_bytes=VMEM_LIMIT)


def _rms(xf, g):
    ms = jnp.mean(xf * xf, axis=-1, keepdims=True)
    return xf * lax.rsqrt(ms + EPS) * g


def _dot(a, b):
    return jnp.dot(a, b, preferred_element_type=F32)


def _low_lanes():
    return lax.broadcasted_iota(jnp.int32, (1, LANES), 1) < HEAD_DIM


def _exp_weights(t):
    return jnp.exp2(t - jnp.max(t, axis=-1, keepdims=True)).astype(BF16)


def _fill_ones_aug(vaug_ref, v_ref, n_blocks):
    ones = jnp.ones((v_ref.shape[1], LANES), BF16)
    for p in range(n_blocks):
        vaug_ref[:, 2 * p * LANES:(2 * p + 1) * LANES] = v_ref[0, :, p * LANES:(p + 1) * LANES]
        vaug_ref[:, (2 * p + 1) * LANES:(2 * p + 2) * LANES] = ones


def _norm_proj_body(x_ref, g_ref, w_ref, cs_ref, o_ref, kn_out, *, n_chunk):
    hn = _rms(x_ref[...], g_ref[...]).astype(BF16)
    lo = _low_lanes()
    lane = lax.broadcasted_iota(jnp.int32, (1, LANES), 1)
    for j in range(0, o_ref.shape[1], n_chunk):
        ob = (_dot(hn, w_ref[:, j:j + n_chunk]) * cs_ref[:, j:j + n_chunk]).astype(o_ref.dtype)
        o_ref[:, j:j + n_chunk] = ob
        if j == n_chunk:
            k2 = jnp.square(ob.astype(F32))
            kn = jnp.zeros((1, LANES), F32)
            for h in range(DIFF_HEADS):
                kh = k2[:, h * LANES:(h + 1) * LANES]
                n1 = jnp.max(jnp.sum(jnp.where(lo, kh, 0.0), axis=-1, keepdims=True), axis=0, keepdims=True)
                n2 = jnp.max(jnp.sum(jnp.where(lo, 0.0, kh), axis=-1, keepdims=True), axis=0, keepdims=True)
                kn = jnp.where(lane == 2 * h, n1, jnp.where(lane == 2 * h + 1, n2, kn))
            kn_out[0] = jnp.broadcast_to(kn, kn_out.shape[1:])


def _norm_proj(x2, g, w, col_scale, *, tm=1024):
    t, d = x2.shape
    n = w.shape[1]
    return pl.pallas_call(
        functools.partial(_norm_proj_body, n_chunk=DIFF_HEADS * LANES),
        grid=(t // tm,),
        in_specs=[pl.BlockSpec((tm, d), lambda i: (i, 0)),
                  pl.BlockSpec((1, d), lambda i: (0, 0)),
                  pl.BlockSpec((d, n), lambda i: (0, 0)),
                  pl.BlockSpec((1, n), lambda i: (0, 0))],
        out_specs=[pl.BlockSpec((tm, n), lambda i: (i, 0)),
                   pl.BlockSpec((1, 8, LANES), lambda i: (i, 0, 0))],
        out_shape=[jax.ShapeDtypeStruct((t, n), BF16),
                   jax.ShapeDtypeStruct((t // tm, 8, LANES), F32)],
        compiler_params=_cparams("parallel"),
        name="norm_proj",
    )(x2, g.reshape(1, d), w, col_scale)


def _diff_attn_body(lvec_ref, q_ref, k_ref, v_ref, kn_ref, g_ref, o_ref, vaug_ref, t_ref, *, tq, lam_init):
    qi = pl.program_id(1)
    lo = _low_lanes()

    @pl.when(qi == 0)
    def _():
        _fill_ones_aug(vaug_ref, v_ref, DIFF_HEADS)

    kmax = jnp.sqrt(jnp.max(kn_ref[:, 0:1, :], axis=0)) * BOUND_SLACK

    s_len = k_ref.shape[1]
    row = qi * tq + lax.broadcasted_iota(jnp.int32, (tq, s_len), 0)
    col = lax.broadcasted_iota(jnp.int32, (tq, s_len), 1)
    dist = jnp.abs(row - col).astype(F32)
    lv = lvec_ref[...]
    lam = (jnp.exp(jnp.sum(lv[0:1] * lv[1:2], axis=-1, keepdims=True))
           - jnp.exp(jnp.sum(lv[2:3] * lv[3:4], axis=-1, keepdims=True)) + lam_init)

    def maps(h):
        q = q_ref[0, :, h * LANES:(h + 1) * LANES]
        k = k_ref[0, :, h * LANES:(h + 1) * LANES]
        zero = jnp.zeros_like(q)
        bias = (-LOG2E * 2.0 ** (-8.0 * (h + 1) / DIFF_HEADS)) * dist
        s1 = lax.dot_general(jnp.where(lo, q, zero), k, NT_DIMS, preferred_element_type=F32)
        s2 = lax.dot_general(jnp.where(lo, zero, q), k, NT_DIMS, preferred_element_type=F32)
        return q, s1, s2, bias

    def finish(h, o1, o2):
        o = o1[:, :LANES] / o1[:, LANES:] - lam * (o2[:, :LANES] / o2[:, LANES:])
        o_ref[0, :, h * LANES:(h + 1) * LANES] = (_rms(o, g_ref[...]) * (1.0 - lam_init)).astype(o_ref.dtype)

    lmin = None
    for h in range(DIFF_HEADS):
        q, s1, s2, bias = maps(h)
        q2 = jnp.square(q.astype(F32))
        b1 = (jnp.sqrt(jnp.sum(jnp.where(lo, q2, 0.0), axis=-1, keepdims=True))
              * kmax[:, 2 * h:2 * h + 1])
        b2 = (jnp.sqrt(jnp.sum(jnp.where(lo, 0.0, q2), axis=-1, keepdims=True))
              * kmax[:, 2 * h + 1:2 * h + 2])
        vaug = vaug_ref[:, 2 * h * LANES:(2 * h + 2) * LANES]
        o1 = _dot(jnp.exp2(s1 + (bias - b1)).astype(BF16), vaug)
        o2 = _dot(jnp.exp2(s2 + (bias - b2)).astype(BF16), vaug)
        finish(h, o1, o2)
        l = jnp.minimum(o1[:, LANES:], o2[:, LANES:])
        lmin = l if lmin is None else jnp.minimum(lmin, l)

    @pl.when(jnp.logical_not(jnp.min(lmin) >= DENOM_MIN))
    def _():
        for h in range(DIFF_HEADS):
            _, s1, s2, bias = maps(h)
            t_ref[0] = s1 + bias
            t_ref[1] = s2 + bias
            vaug = vaug_ref[:, 2 * h * LANES:(2 * h + 2) * LANES]
            finish(h, _dot(_exp_weights(t_ref[0]), vaug), _dot(_exp_weights(t_ref[1]), vaug))


def _diff_attention(proj, kn, lvec, subln_g, *, lam_init, tq=512):
    b, s, _ = proj.shape
    w = DIFF_HEADS * LANES
    return pl.pallas_call(
        functools.partial(_diff_attn_body, tq=tq, lam_init=lam_init),
        grid=(b, s // tq),
        in_specs=[pl.BlockSpec((4, HEAD_DIM), lambda bi, qi: (0, 0)),
                  pl.BlockSpec((1, tq, w), lambda bi, qi: (bi, qi, 0)),
                  pl.BlockSpec((1, s, w), lambda bi, qi: (bi, 0, 1)),
                  pl.BlockSpec((1, s, w), lambda bi, qi: (bi, 0, 2)),
                  pl.BlockSpec((kn.shape[0] // b, 8, LANES), lambda bi, qi: (bi, 0, 0)),
                  pl.BlockSpec((1, LANES), lambda bi, qi: (0, 0))],
        out_specs=pl.BlockSpec((1, tq, w), lambda bi, qi: (bi, qi, 0)),
        out_shape=jax.ShapeDtypeStruct((b, s, w), BF16),
        scratch_shapes=[pltpu.VMEM((s, 2 * w), BF16), pltpu.VMEM((2, tq, s), F32)],
        compiler_params=_cparams("parallel", "arbitrary"),
        name="diff_attn",
    )(lvec, proj, proj, proj, kn, subln_g.reshape(1, LANES))


def _na_bias_table(rpb, kh):
    qc = np.arange(GRID_W)[:, None]
    kc = np.arange(GRID_W)[None, :]
    wstart = np.clip(qc - NA_WIN_W // 2, 0, GRID_W - NA_WIN_W)
    valid = (kc >= wstart) & (kc < wstart + NA_WIN_W)
    dc = kc - qc + NA_WIN_W - 1
    onehot = ((dc[None] == np.arange(2 * NA_WIN_W - 1)[:, None, None]) & valid[None]).astype(np.float32)
    toep = jnp.einsum("hrc,cqk->hrqk", rpb.astype(F32) * LOG2E, onehot, precision=lax.Precision.HIGHEST)
    toep = jnp.where(valid[None, None], toep, NEG_INF)
    per_d = [toep[:, NA_WIN_H - 1 - d:NA_WIN_H - 1 - d + kh].transpose(0, 2, 1, 3) for d in range(kh)]
    return jnp.stack(per_d, axis=1).reshape(rpb.shape[0], kh, GRID_W, kh * GRID_W)


def _na_body(q_ref, k_ref, v_ref, bias_ref, o_ref, vaug_ref, *, rows_n, kh):
    lo = _low_lanes()
    _fill_ones_aug(vaug_ref, v_ref, 1)
    for r in range(rows_n):
        rs = min(max(r - kh // 2, 0), rows_n - kh)
        q = q_ref[0, r * GRID_W:(r + 1) * GRID_W, :]
        kwin = k_ref[0, rs * GRID_W:(rs + kh) * GRID_W, :]
        vwin = vaug_ref[rs * GRID_W:(rs + kh) * GRID_W, :]
        zero = jnp.zeros_like(q)
        q2 = jnp.concatenate([jnp.where(lo, q, zero), jnp.where(lo, zero, q)], axis=0)
        bias = bias_ref[:, r - rs].reshape(2 * GRID_W, kh * GRID_W)
        t = lax.dot_general(q2, kwin, NT_DIMS, preferred_element_type=F32) + bias
        o2 = _dot(_exp_weights(t), vwin)
        o2 = o2[:, :LANES] / o2[:, LANES:]
        o_ref[0, r * GRID_W:(r + 1) * GRID_W, :] = jnp.where(lo, o2[:GRID_W], o2[GRID_W:]).astype(o_ref.dtype)


def _na_attention(proj, rpb, *, col0):
    b, s, _ = proj.shape
    rows_n = s // GRID_W
    kh = min(NA_WIN_H, rows_n)
    pairs = NA_HEADS // 2
    bias = _na_bias_table(rpb, kh)
    return pl.pallas_call(
        functools.partial(_na_body, rows_n=rows_n, kh=kh),
        grid=(b, pairs),
        in_specs=[pl.BlockSpec((1, s, LANES), lambda bi, p: (bi, 0, col0 + p)),
                  pl.BlockSpec((1, s, LANES), lambda bi, p: (bi, 0, col0 + pairs + p)),
                  pl.BlockSpec((1, s, LANES), lambda bi, p: (bi, 0, col0 + 2 * pairs + p)),
                  pl.BlockSpec((2, kh, GRID_W, kh * GRID_W), lambda bi, p: (p, 0, 0, 0))],
        out_specs=pl.BlockSpec((1, s, LANES), lambda bi, p: (bi, 0, p)),
        out_shape=jax.ShapeDtypeStruct((b, s, pairs * LANES), BF16),
        scratch_shapes=[pltpu.VMEM((s, 2 * LANES), BF16)],
        compiler_params=_cparams("parallel", "parallel"),
        name="na_attn",
    )(proj, proj, proj, bias)


HALO = 16


def _mix_ffn_body(*refs, n_act, tiles_per_seq, final_norm):
    acts = refs[:3 * n_act]
    wos = refs[3 * n_act:4 * n_act]
    x_ref, xp_ref, xn_ref, g_ref, wg_ref, wv_ref, cw_ref, cb_ref, wd_ref, fg_ref, o_ref = refs[4 * n_act:]
    pos = pl.program_id(0) % tiles_per_seq
    tm = x_ref.shape[0]
    xe = jnp.concatenate([xp_ref[...], x_ref[...], xn_ref[...]], axis=0)
    for k in range(n_act):
        a_ref, ap_ref, an_ref = acts[3 * k:3 * k + 3]
        xe = xe + _dot(jnp.concatenate([ap_ref[...], a_ref[...], an_ref[...]], axis=0), wos[k][...])
    row = lax.broadcasted_iota(jnp.int32, (tm + 2 * HALO, 1), 0)
    first_kept = jnp.where(pos == 0, HALO, 0)
    end_kept = jnp.where(pos == tiles_per_seq - 1, HALO + tm, tm + 2 * HALO)
    xe = jnp.where((row >= first_kept) & (row < end_kept), xe, 0.0)
    y = _rms(xe, g_ref[...])
    a_ext = _dot(y.astype(BF16), wg_ref[...])
    cw = cw_ref[...]
    a = (cw[0:1] * a_ext[HALO - 1:HALO - 1 + tm] + cw[1:2] * a_ext[HALO:HALO + tm]
         + cw[2:3] * a_ext[HALO + 1:HALO + 1 + tm] + cb_ref[...])
    gate = 0.5 * a * (1.0 + lax.erf(a * SQRT_HALF))
    val = _dot(y[HALO:HALO + tm].astype(BF16), wv_ref[...])
    out = xe[HALO:HALO + tm] + _dot((gate * val).astype(BF16), wd_ref[...])
    o_ref[...] = _rms(out, fg_ref[...]) if final_norm else out


def _mix_ffn(acts, w_out, x2, s, g, w_gate, w_val, conv_w, conv_b, w_down, final_g, *, layer, final_norm, tm=512):
    t, d = x2.shape
    dff = w_gate.shape[2]
    wa = acts[0].shape[1]
    assert all(a.shape[1] == wa for a in acts) and wa * len(acts) == w_out.shape[0]
    hb = tm // HALO
    last = t // HALO - 1

    def tiles(width):
        return [pl.BlockSpec((tm, width), lambda i: (i, 0)),
                pl.BlockSpec((HALO, width), lambda i: (jnp.maximum(i * hb - 1, 0), 0)),
                pl.BlockSpec((HALO, width), lambda i: (jnp.minimum((i + 1) * hb, last), 0))]

    def const(shape, row_block=0):
        return pl.BlockSpec(shape, lambda i: (row_block, 0), pipeline_mode=pl.Buffered(1))

    def slab(w):
        return pl.BlockSpec((None,) + w.shape[1:], lambda i: (layer, 0, 0), pipeline_mode=pl.Buffered(1))

    consts = (g.reshape(1, d), w_gate, w_val, conv_w, conv_b.reshape(1, dff), w_down, final_g.reshape(1, d))
    act_args = [a for act in acts for a in (act, act, act)]
    act_specs = [sp for _ in acts for sp in tiles(wa)]
    return pl.pallas_call(
        functools.partial(_mix_ffn_body, n_act=len(acts), tiles_per_seq=s // tm, final_norm=final_norm),
        grid=(t // tm,),
        in_specs=(act_specs + [const((wa, d), k) for k in range(len(acts))] + tiles(d)
                  + [slab(c) if c.ndim == 3 else const(c.shape) for c in consts]),
        out_specs=pl.BlockSpec((tm, d), lambda i: (i, 0)),
        out_shape=jax.ShapeDtypeStruct((t, d), F32),
        compiler_params=_cparams("parallel"),
        name="mix_ffn",
    )(*act_args, *([w_out] * len(acts)), x2, x2, x2, *consts)


def _mla_prep_body(x_ref, g_ref, wdq_ref, qg_ref, wuq_ref, wuqs_ref, wdkv_ref, kvg_ref, wk_ref, wvt_ref,
                   ct_ref, st_ref, q_out, k_out, vt_out, kn_out, *, heads_per_group):
    hn = _rms(x_ref[...], g_ref[...]).astype(BF16)
    cq = _rms(_dot(hn, wdq_ref[...]), qg_ref[...]).astype(BF16)
    kva = _dot(hn, wdkv_ref[...])
    ckv = _rms(kva[:, :MLA_KV_RANK], kvg_ref[...]).astype(BF16)
    ct = ct_ref[...]
    st = st_ref[...]
    kr = kva[:, MLA_KV_RANK:MLA_KV_RANK + LANES]
    kr_sw = kva[:, MLA_KV_RANK + LANES:]
    roped = kr * ct[:, :LANES] + kr_sw * st[:, :LANES]
    shift_lane = lax.broadcasted_iota(jnp.int32, (1, LANES), 1) == MLA_AUG_LANE
    roped = jnp.where(shift_lane, 1.0, roped)
    roped2 = jnp.concatenate([roped, roped], axis=1)
    w2 = 2 * LANES
    lane = lax.broadcasted_iota(jnp.int32, (1, LANES), 1)
    kn_rows = [jnp.zeros((1, LANES), F32) for _ in range(MLA_HEADS // heads_per_group)]
    for j in range(0, q_out.shape[1], w2):
        qh = _dot(cq, wuq_ref[:, j:j + w2])
        qs = _dot(cq, wuqs_ref[:, j:j + w2])
        q_out[:, j:j + w2] = ((qh * ct + qs * st) * MLA_SOFTMAX_SCALE).astype(q_out.dtype)
        kb = (_dot(ckv, wk_ref[:, j:j + w2]) + roped2).astype(k_out.dtype)
        k_out[:, j:j + w2] = kb
        k2 = jnp.square(kb.astype(F32))
        for half in range(2):
            hh = j // LANES + half
            n = jnp.sum(jnp.where(shift_lane, 0.0, k2[:, half * LANES:(half + 1) * LANES]), axis=-1, keepdims=True)
            n = jnp.max(n, axis=0, keepdims=True)
            grp = hh // heads_per_group
            kn_rows[grp] = jnp.where(lane == hh % heads_per_group, n, kn_rows[grp])
    for grp, row in enumerate(kn_rows):
        kn_out[0, grp] = jnp.broadcast_to(row, kn_out.shape[2:])
    vt = lax.dot_general(wvt_ref[...], ckv, NT_DIMS, preferred_element_type=F32)
    ones_row = lax.broadcasted_iota(jnp.int32, (vt.shape[0], 1), 0) % MLA_VT_ROWS == MLA_V
    vt_out[0] = jnp.where(ones_row, 1.0, vt).astype(vt_out.dtype)


def _mla_prep(x2, s, g, wdq, qg, wuq, wuqs, wdkv, kvg, wk, wvt, ct, st, *, heads_per_group, tm=512):
    t, d = x2.shape
    nblk = s // tm
    full = lambda a: pl.BlockSpec(a.shape, lambda i: (0, 0))
    hw = MLA_HEADS * LANES
    groups = MLA_HEADS // heads_per_group
    return pl.pallas_call(
        functools.partial(_mla_prep_body, heads_per_group=heads_per_group),
        grid=(t // tm,),
        in_specs=[pl.BlockSpec((tm, d), lambda i: (i, 0)), full(g), full(wdq), full(qg), full(wuq), full(wuqs),
                  full(wdkv), full(kvg), full(wk), full(wvt),
                  pl.BlockSpec((tm, 2 * LANES), lambda i: (i % nblk, 0)),
                  pl.BlockSpec((tm, 2 * LANES), lambda i: (i % nblk, 0))],
        out_specs=[pl.BlockSpec((tm, hw), lambda i: (i, 0)),
                   pl.BlockSpec((tm, hw), lambda i: (i, 0)),
                   pl.BlockSpec((1, MLA_HEADS * MLA_VT_ROWS, tm), lambda i: (i // nblk, 0, i % nblk)),
                   pl.BlockSpec((1, groups, 8, LANES), lambda i: (i, 0, 0, 0))],
        out_shape=[jax.ShapeDtypeStruct((t, hw), BF16),
                   jax.ShapeDtypeStruct((t, hw), BF16),
                   jax.ShapeDtypeStruct((t // s, MLA_HEADS * MLA_VT_ROWS, s), BF16),
                   jax.ShapeDtypeStruct((t // tm, groups, 8, LANES), F32)],
        compiler_params=_cparams("parallel"),
        name="mla_prep",
    )(x2, g, wdq, qg, wuq, wuqs, wdkv, kvg, wk, wvt, ct, st)


def _mla_attn_body(q_ref, k_ref, vt_ref, kn_ref, o_ref, t_ref, *, heads):
    shift_lane = lax.broadcasted_iota(jnp.int32, (1, LANES), 1) == MLA_AUG_LANE
    kmax = jnp.sqrt(jnp.max(kn_ref[:, 0, 0:1, :], axis=0))

    def values(hh, et, halves):
        ot = _dot(vt_ref[0, hh * MLA_VT_ROWS:(hh + 1) * MLA_VT_ROWS, :], et)
        denom = ot[MLA_V:MLA_V + 1]
        halves.append(ot[:MLA_V] / denom)
        if hh % 2 == 1:
            p = hh // 2
            o_ref[0, :, p * LANES:(p + 1) * LANES] = jnp.concatenate(halves, axis=0).T.astype(o_ref.dtype)
            halves.clear()
        return denom

    lmin = None
    halves = []
    for hh in range(heads):
        q = q_ref[0, :, hh * LANES:(hh + 1) * LANES]
        k = k_ref[0, :, hh * LANES:(hh + 1) * LANES]
        qn = jnp.sqrt(jnp.sum(jnp.square(q.astype(F32)), axis=-1, keepdims=True))
        shift = (qn * (kmax[:, hh:hh + 1] * -BOUND_SLACK)).astype(BF16)
        tt = lax.dot_general(k, jnp.where(shift_lane, shift, q), NT_DIMS, preferred_element_type=F32)
        denom = values(hh, jnp.exp2(tt).astype(BF16), halves)
        lmin = denom if lmin is None else jnp.minimum(lmin, denom)

    @pl.when(jnp.logical_not(jnp.min(lmin) >= DENOM_MIN))
    def _():
        halves = []
        for hh in range(heads):
            q = q_ref[0, :, hh * LANES:(hh + 1) * LANES]
            k = k_ref[0, :, hh * LANES:(hh + 1) * LANES]
            t_ref[...] = lax.dot_general(k, q, NT_DIMS, preferred_element_type=F32)
            m = jnp.max(t_ref[...], axis=0, keepdims=True)
            values(hh, jnp.exp2(t_ref[...] - m).astype(BF16), halves)


def _mla_attention(q, k, vt, kn, *, heads, tq=512):
    b, s, _ = q.shape
    groups = MLA_HEADS // heads
    return pl.pallas_call(
        functools.partial(_mla_attn_body, heads=heads),
        grid=(b, groups, s // tq),
        in_specs=[pl.BlockSpec((1, tq, heads * LANES), lambda bi, g, qi: (bi, qi, g)),
                  pl.BlockSpec((1, s, heads * LANES), lambda bi, g, qi: (bi, 0, g)),
                  pl.BlockSpec((1, heads * MLA_VT_ROWS, s), lambda bi, g, qi: (bi, g, 0)),
                  pl.BlockSpec((kn.shape[0] // b, 1, 8, LANES), lambda bi, g, qi: (bi, g, 0, 0))],
        out_specs=pl.BlockSpec((1, tq, heads * MLA_V), lambda bi, g, qi: (bi, qi, g)),
        out_shape=jax.ShapeDtypeStruct((b, s, MLA_HEADS * MLA_V), BF16),
        scratch_shapes=[pltpu.VMEM((s, tq), F32)],
        compiler_params=_cparams("parallel", "parallel", "parallel"),
        name="mla_attn",
    )(q, k, vt, kn)


def _mla_weights(w_uq, w_dkv, w_ukv):
    f = MLA_NOPE + MLA_ROPE
    half = MLA_ROPE // 2
    r = w_uq.shape[0]
    uq = w_uq.reshape(r, MLA_HEADS, f)
    zq = jnp.zeros((r, MLA_HEADS, LANES - f), w_uq.dtype)
    wuq = jnp.concatenate([uq, zq], axis=-1).reshape(r, MLA_HEADS * LANES)
    wuqs = jnp.concatenate([jnp.zeros((r, MLA_HEADS, MLA_NOPE), w_uq.dtype),
                            uq[..., MLA_NOPE + half:], uq[..., MLA_NOPE:MLA_NOPE + half], zq],
                           axis=-1).reshape(r, MLA_HEADS * LANES)
    d = w_dkv.shape[0]
    kr = w_dkv[:, MLA_KV_RANK:]
    z64 = jnp.zeros((d, MLA_NOPE), w_dkv.dtype)
    z32 = jnp.zeros((d, LANES - f), w_dkv.dtype)
    wdkv = jnp.concatenate([w_dkv[:, :MLA_KV_RANK], z64, kr, z32,
                            z64, kr[:, half:], kr[:, :half], z32], axis=-1)
    c = w_ukv.shape[0]
    ukv = w_ukv.reshape(c, MLA_HEADS, MLA_NOPE + MLA_V)
    wk = jnp.concatenate([ukv[..., :MLA_NOPE], jnp.zeros((c, MLA_HEADS, LANES - MLA_NOPE), w_ukv.dtype)],
                         axis=-1).reshape(c, MLA_HEADS * LANES)
    wvt = jnp.concatenate([ukv[..., MLA_NOPE:].transpose(1, 2, 0),
                           jnp.zeros((MLA_HEADS, MLA_VT_ROWS - MLA_V, c), w_ukv.dtype)], axis=1)
    wvt = wvt.reshape(MLA_HEADS * MLA_VT_ROWS, c)
    return wuq.astype(BF16), wuqs.astype(BF16), wdkv.astype(BF16), wk.astype(BF16), wvt.astype(BF16)


def _rope_lane_tables(s):
    inv = ROPE_THETA ** (-jnp.arange(0, MLA_ROPE, 2, dtype=F32) / MLA_ROPE)
    ang = jnp.arange(s, dtype=F32)[:, None] * inv[None, :]
    cos, sin = jnp.cos(ang), jnp.sin(ang)
    one = jnp.ones((s, MLA_NOPE), F32)
    z64 = jnp.zeros((s, MLA_NOPE), F32)
    z32 = jnp.zeros((s, LANES - MLA_NOPE - MLA_ROPE), F32)
    ct = jnp.concatenate([one, cos, cos, z32], axis=-1)
    st = jnp.concatenate([z64, -sin, sin, z32], axis=-1)
    return jnp.tile(ct, (1, 2)), jnp.tile(st, (1, 2))


def kernel(x, mix_norm_e, w_in_e, diff_lq1, diff_lk1, diff_lq2, diff_lk2, diff_subln_g, na_rpb, w_out_e,
           mix_norm_o, w_dq, q_norm_g, w_uq, w_dkv, kv_norm_g, w_ukv, w_o_mla, ffn_norm_g, w_ffn_gate,
           w_ffn_val, ffn_conv_w, ffn_conv_b, w_ffn_down, final_norm_g):
    b, s, d = x.shape
    t = b * s
    depth = ffn_norm_g.shape[0]
    wg_all, wv_all, wd_all = w_ffn_gate.astype(BF16), w_ffn_val.astype(BF16), w_ffn_down.astype(BF16)
    for i in range(depth):
        j = i // 2
        x2 = x.reshape(t, d)
        if i % 2 == 0:
            lam_init = 0.8 - 0.6 * math.exp(-0.3 * i)
            wq = DIFF_HEADS * LANES
            col_scale = np.ones((1, w_in_e.shape[-1]), np.float32)
            col_scale[:, :wq] = SOFTMAX_SCALE
            col_scale[:, 3 * wq:3 * wq + NA_HEADS * HEAD_DIM] = SOFTMAX_SCALE
            proj, kn_d = _norm_proj(x2, mix_norm_e[j], w_in_e[j].astype(BF16), jnp.asarray(col_scale))
            proj = proj.reshape(b, s, -1)
            lvec = jnp.stack([diff_lq1[j], diff_lk1[j], diff_lq2[j], diff_lk2[j]]).astype(F32)
            a = _diff_attention(proj, kn_d, lvec, diff_subln_g[j], lam_init=lam_init)
            nb = _na_attention(proj, na_rpb[j], col0=3 * DIFF_HEADS)
            acts, w_out = [a.reshape(t, -1), nb.reshape(t, -1)], w_out_e[j].astype(BF16)
        else:
            wuq, wuqs, wdkv, wk, wvt = _mla_weights(w_uq[j], w_dkv[j], w_ukv[j])
            ct, st = _rope_lane_tables(s)
            q, k, vt, kn = _mla_prep(x2, s, mix_norm_o[j].reshape(1, d), w_dq[j].astype(BF16),
                                     q_norm_g[j].reshape(1, -1), wuq, wuqs, wdkv, kv_norm_g[j].reshape(1, -1),
                                     wk, wvt, ct, st, heads_per_group=MLA_HEADS_PER_STEP)
            o = _mla_attention(q.reshape(b, s, -1), k.reshape(b, s, -1), vt, kn, heads=MLA_HEADS_PER_STEP)
            acts, w_out = [o.reshape(t, -1)], w_o_mla[j].astype(BF16)
        x = _mix_ffn(acts, w_out, x2, s, ffn_norm_g[i], wg_all, wv_all, ffn_conv_w[i], ffn_conv_b[i], wd_all,
                     final_norm_g, layer=i, final_norm=(i == depth - 1)).reshape(b, s, d)
    return x
```

```python
import functools
import math

import numpy as np
import jax
import jax.numpy as jnp
from jax import lax
from jax.experimental import pallas as pl
from jax.experimental.pallas import tpu as pltpu

F32 = jnp.float32
BF16 = jnp.bfloat16

EPS = 1e-6
NEG_INF = -1e30
GRID_W = 64
HEAD_DIM = 64
DIFF_HEADS = 4
NA_HEADS = 8
NA_WIN_H = 8
NA_WIN_W = 16
MLA_HEADS = 16
MLA_NOPE = 64
MLA_ROPE = 32
MLA_V = 64
MLA_KV_RANK = 256
ROPE_THETA = 10000.0

LANES = 128
VMEM_LIMIT = 56 * 1024 * 1024
NT_DIMS = (((1,), (1,)), ((), ()))
LOG2E = math.log2(math.e)
SQRT_HALF = np.float32(math.sqrt(0.5))
SOFTMAX_SCALE = HEAD_DIM ** -0.5 * LOG2E
MLA_SOFTMAX_SCALE = (MLA_NOPE + MLA_ROPE) ** -0.5 * LOG2E
BOUND_SLACK = 1.0 + 2.0 ** -6
DENOM_MIN = 2.0 ** -60
MLA_AUG_LANE = MLA_NOPE + MLA_ROPE
MLA_HEADS_PER_STEP = 8
MLA_VT_ROWS = 80


def _cparams(*sem):
    return pltpu.CompilerParams(dimension_semantics=sem, vmem_limit_bytes=VMEM_LIMIT)


def _rms(xf, g):
    ms = jnp.mean(xf * xf, axis=-1, keepdims=True)
    return xf * lax.rsqrt(ms + EPS) * g


def _dot(a, b):
    return jnp.dot(a, b, preferred_element_type=F32)


def _low_lanes():
    return lax.broadcasted_iota(jnp.int32, (1, LANES), 1) < HEAD_DIM


def _exp_weights(t):
    return jnp.exp2(t - jnp.max(t, axis=-1, keepdims=True)).astype(BF16)


def _fill_ones_aug(vaug_ref, v_ref, n_blocks):
    ones = jnp.ones((v_ref.shape[1], LANES), BF16)
    for p in range(n_blocks):
        vaug_ref[:, 2 * p * LANES:(2 * p + 1) * LANES] = v_ref[0, :, p * LANES:(p + 1) * LANES]
        vaug_ref[:, (2 * p + 1) * LANES:(2 * p + 2) * LANES] = ones


def _norm_proj_body(x_ref, g_ref, w_ref, cs_ref, o_ref, kn_out, *, n_chunk):
    hn = _rms(x_ref[...], g_ref[...]).astype(BF16)
    lo = _low_lanes()
    lane = lax.broadcasted_iota(jnp.int32, (1, LANES), 1)
    for j in range(0, o_ref.shape[1], n_chunk):
        ob = (_dot(hn, w_ref[:, j:j + n_chunk]) * cs_ref[:, j:j + n_chunk]).astype(o_ref.dtype)
        o_ref[:, j:j + n_chunk] = ob
        if j == n_chunk:
            k2 = jnp.square(ob.astype(F32))
            kn = jnp.zeros((1, LANES), F32)
            for h in range(DIFF_HEADS):
                kh = k2[:, h * LANES:(h + 1) * LANES]
                n1 = jnp.max(jnp.sum(jnp.where(lo, kh, 0.0), axis=-1, keepdims=True), axis=0, keepdims=True)
                n2 = jnp.max(jnp.sum(jnp.where(lo, 0.0, kh), axis=-1, keepdims=True), axis=0, keepdims=True)
                kn = jnp.where(lane == 2 * h, n1, jnp.where(lane == 2 * h + 1, n2, kn))
            kn_out[0] = jnp.broadcast_to(kn, kn_out.shape[1:])


def _norm_proj(x2, g, w, col_scale, *, tm=1024):
    t, d = x2.shape
    n = w.shape[1]
    return pl.pallas_call(
        functools.partial(_norm_proj_body, n_chunk=DIFF_HEADS * LANES),
        grid=(t // tm,),
        in_specs=[pl.BlockSpec((tm, d), lambda i: (i, 0)),
                  pl.BlockSpec((1, d), lambda i: (0, 0)),
                  pl.BlockSpec((d, n), lambda i: (0, 0)),
                  pl.BlockSpec((1, n), lambda i: (0, 0))],
        out_specs=[pl.BlockSpec((tm, n), lambda i: (i, 0)),
                   pl.BlockSpec((1, 8, LANES), lambda i: (i, 0, 0))],
        out_shape=[jax.ShapeDtypeStruct((t, n), BF16),
                   jax.ShapeDtypeStruct((t // tm, 8, LANES), F32)],
        compiler_params=_cparams("parallel"),
        name="norm_proj",
    )(x2, g.reshape(1, d), w, col_scale)


def _diff_attn_body(lvec_ref, q_ref, k_ref, v_ref, kn_ref, g_ref, o_ref, vaug_ref, t_ref, *, tq, lam_init):
    qi = pl.program_id(1)
    lo = _low_lanes()

    @pl.when(qi == 0)
    def _():
        _fill_ones_aug(vaug_ref, v_ref, DIFF_HEADS)

    kmax = jnp.sqrt(jnp.max(kn_ref[:, 0:1, :], axis=0)) * BOUND_SLACK

    s_len = k_ref.shape[1]
    row = qi * tq + lax.broadcasted_iota(jnp.int32, (tq, s_len), 0)
    col = lax.broadcasted_iota(jnp.int32, (tq, s_len), 1)
    dist = jnp.abs(row - col).astype(F32)
    lv = lvec_ref[...]
    lam = (jnp.exp(jnp.sum(lv[0:1] * lv[1:2], axis=-1, keepdims=True))
           - jnp.exp(jnp.sum(lv[2:3] * lv[3:4], axis=-1, keepdims=True)) + lam_init)

    def maps(h):
        q = q_ref[0, :, h * LANES:(h + 1) * LANES]
        k = k_ref[0, :, h * LANES:(h + 1) * LANES]
        zero = jnp.zeros_like(q)
        bias = (-LOG2E * 2.0 ** (-8.0 * (h + 1) / DIFF_HEADS)) * dist
        s1 = lax.dot_general(jnp.where(lo, q, zero), k, NT_DIMS, preferred_element_type=F32)
        s2 = lax.dot_general(jnp.where(lo, zero, q), k, NT_DIMS, preferred_element_type=F32)
        return q, s1, s2, bias

    def finish(h, o1, o2):
        o = o1[:, :LANES] / o1[:, LANES:] - lam * (o2[:, :LANES] / o2[:, LANES:])
        o_ref[0, :, h * LANES:(h + 1) * LANES] = (_rms(o, g_ref[...]) * (1.0 - lam_init)).astype(o_ref.dtype)

    lmin = None
    for h in range(DIFF_HEADS):
        q, s1, s2, bias = maps(h)
        q2 = jnp.square(q.astype(F32))
        b1 = (jnp.sqrt(jnp.sum(jnp.where(lo, q2, 0.0), axis=-1, keepdims=True))
              * kmax[:, 2 * h:2 * h + 1])
        b2 = (jnp.sqrt(jnp.sum(jnp.where(lo, 0.0, q2), axis=-1, keepdims=True))
              * kmax[:, 2 * h + 1:2 * h + 2])
        vaug = vaug_ref[:, 2 * h * LANES:(2 * h + 2) * LANES]
        o1 = _dot(jnp.exp2(s1 + (bias - b1)).astype(BF16), vaug)
        o2 = _dot(jnp.exp2(s2 + (bias - b2)).astype(BF16), vaug)
        finish(h, o1, o2)
        l = jnp.minimum(o1[:, LANES:], o2[:, LANES:])
        lmin = l if lmin is None else jnp.minimum(lmin, l)

    @pl.when(jnp.logical_not(jnp.min(lmin) >= DENOM_MIN))
    def _():
        for h in range(DIFF_HEADS):
            _, s1, s2, bias = maps(h)
            t_ref[0] = s1 + bias
            t_ref[1] = s2 + bias
            vaug = vaug_ref[:, 2 * h * LANES:(2 * h + 2) * LANES]
            finish(h, _dot(_exp_weights(t_ref[0]), vaug), _dot(_exp_weights(t_ref[1]), vaug))


def _diff_attention(proj, kn, lvec, subln_g, *, lam_init, tq=512):
    b, s, _ = proj.shape
    w = DIFF_HEADS * LANES
    return pl.pallas_call(
        functools.partial(_diff_attn_body, tq=tq, lam_init=lam_init),
        grid=(b, s // tq),
        in_specs=[pl.BlockSpec((4, HEAD_DIM), lambda bi, qi: (0, 0)),
                  pl.BlockSpec((1, tq, w), lambda bi, qi: (bi, qi, 0)),
                  pl.BlockSpec((1, s, w), lambda bi, qi: (bi, 0, 1)),
                  pl.BlockSpec((1, s, w), lambda bi, qi: (bi, 0, 2)),
                  pl.BlockSpec((kn.shape[0] // b, 8, LANES), lambda bi, qi: (bi, 0, 0)),
                  pl.BlockSpec((1, LANES), lambda bi, qi: (0, 0))],
        out_specs=pl.BlockSpec((1, tq, w), lambda bi, qi: (bi, qi, 0)),
        out_shape=jax.ShapeDtypeStruct((b, s, w), BF16),
        scratch_shapes=[pltpu.VMEM((s, 2 * w), BF16), pltpu.VMEM((2, tq, s), F32)],
        compiler_params=_cparams("parallel", "arbitrary"),
        name="diff_attn",
    )(lvec, proj, proj, proj, kn, subln_g.reshape(1, LANES))


def _na_bias_table(rpb, kh):
    qc = np.arange(GRID_W)[:, None]
    kc = np.arange(GRID_W)[None, :]
    wstart = np.clip(qc - NA_WIN_W // 2, 0, GRID_W - NA_WIN_W)
    valid = (kc >= wstart) & (kc < wstart + NA_WIN_W)
    dc = kc - qc + NA_WIN_W - 1
    onehot = ((dc[None] == np.arange(2 * NA_WIN_W - 1)[:, None, None]) & valid[None]).astype(np.float32)
    toep = jnp.einsum("hrc,cqk->hrqk", rpb.astype(F32) * LOG2E, onehot, precision=lax.Precision.HIGHEST)
    toep = jnp.where(valid[None, None], toep, NEG_INF)
    per_d = [toep[:, NA_WIN_H - 1 - d:NA_WIN_H - 1 - d + kh].transpose(0, 2, 1, 3) for d in range(kh)]
    return jnp.stack(per_d, axis=1).reshape(rpb.shape[0], kh, GRID_W, kh * GRID_W)


def _na_body(q_ref, k_ref, v_ref, bias_ref, o_ref, vaug_ref, *, rows_n, kh):
    lo = _low_lanes()
    _fill_ones_aug(vaug_ref, v_ref, 1)
    for r in range(rows_n):
        rs = min(max(r - kh // 2, 0), rows_n - kh)
        q = q_ref[0, r * GRID_W:(r + 1) * GRID_W, :]
        kwin = k_ref[0, rs * GRID_W:(rs + kh) * GRID_W, :]
        vwin = vaug_ref[rs * GRID_W:(rs + kh) * GRID_W, :]
        zero = jnp.zeros_like(q)
        q2 = jnp.concatenate([jnp.where(lo, q, zero), jnp.where(lo, zero, q)], axis=0)
        bias = bias_ref[:, r - rs].reshape(2 * GRID_W, kh * GRID_W)
        t = lax.dot_general(q2, kwin, NT_DIMS, preferred_element_type=F32) + bias
        o2 = _dot(_exp_weights(t), vwin)
        o2 = o2[:, :LANES] / o2[:, LANES:]
        o_ref[0, r * GRID_W:(r + 1) * GRID_W, :] = jnp.where(lo, o2[:GRID_W], o2[GRID_W:]).astype(o_ref.dtype)


def _na_attention(proj, rpb, *, col0):
    b, s, _ = proj.shape
    rows_n = s // GRID_W
    kh = min(NA_WIN_H, rows_n)
    pairs = NA_HEADS // 2
    bias = _na_bias_table(rpb, kh)
    return pl.pallas_call(
        functools.partial(_na_body, rows_n=rows_n, kh=kh),
        grid=(b, pairs),
        in_specs=[pl.BlockSpec((1, s, LANES), lambda bi, p: (bi, 0, col0 + p)),
                  pl.BlockSpec((1, s, LANES), lambda bi, p: (bi, 0, col0 + pairs + p)),
                  pl.BlockSpec((1, s, LANES), lambda bi, p: (bi, 0, col0 + 2 * pairs + p)),
                  pl.BlockSpec((2, kh, GRID_W, kh * GRID_W), lambda bi, p: (p, 0, 0, 0))],
        out_specs=pl.BlockSpec((1, s, LANES), lambda bi, p: (bi, 0, p)),
        out_shape=jax.ShapeDtypeStruct((b, s, pairs * LANES), BF16),
        scratch_shapes=[pltpu.VMEM((s, 2 * LANES), BF16)],
        compiler_params=_cparams("parallel", "parallel"),
        name="na_attn",
    )(proj, proj, proj, bias)


HALO = 16


def _mix_ffn_body(*refs, n_act, tiles_per_seq, final_norm):
    acts = refs[:3 * n_act]
    wos = refs[3 * n_act:4 * n_act]
    x_ref, xp_ref, xn_ref, g_ref, wg_ref, wv_ref, cw_ref, cb_ref, wd_ref, fg_ref, o_ref = refs[4 * n_act:]
    pos = pl.program_id(0) % tiles_per_seq
    tm = x_ref.shape[0]
    xe = jnp.concatenate([xp_ref[...], x_ref[...], xn_ref[...]], axis=0)
    for k in range(n_act):
        a_ref, ap_ref, an_ref = acts[3 * k:3 * k + 3]
        xe = xe + _dot(jnp.concatenate([ap_ref[...], a_ref[...], an_ref[...]], axis=0), wos[k][...])
    row = lax.broadcasted_iota(jnp.int32, (tm + 2 * HALO, 1), 0)
    first_kept = jnp.where(pos == 0, HALO, 0)
    end_kept = jnp.where(pos == tiles_per_seq - 1, HALO + tm, tm + 2 * HALO)
    xe = jnp.where((row >= first_kept) & (row < end_kept), xe, 0.0)
    y = _rms(xe, g_ref[...])
    a_ext = _dot(y.astype(BF16), wg_ref[...])
    cw = cw_ref[...]
    a = (cw[0:1] * a_ext[HALO - 1:HALO - 1 + tm] + cw[1:2] * a_ext[HALO:HALO + tm]
         + cw[2:3] * a_ext[HALO + 1:HALO + 1 + tm] + cb_ref[...])
    gate = 0.5 * a * (1.0 + lax.erf(a * SQRT_HALF))
    val = _dot(y[HALO:HALO + tm].astype(BF16), wv_ref[...])
    out = xe[HALO:HALO + tm] + _dot((gate * val).astype(BF16), wd_ref[...])
    o_ref[...] = _rms(out, fg_ref[...]) if final_norm else out


def _mix_ffn(acts, w_out, x2, s, g, w_gate, w_val, conv_w, conv_b, w_down, final_g, *, layer, final_norm, tm=512):
    t, d = x2.shape
    dff = w_gate.shape[2]
    wa = acts[0].shape[1]
    assert all(a.shape[1] == wa for a in acts) and wa * len(acts) == w_out.shape[0]
    hb = tm // HALO
    last = t // HALO - 1

    def tiles(width):
        return [pl.BlockSpec((tm, width), lambda i: (i, 0)),
                pl.BlockSpec((HALO, width), lambda i: (jnp.maximum(i * hb - 1, 0), 0)),
                pl.BlockSpec((HALO, width), lambda i: (jnp.minimum((i + 1) * hb, last), 0))]

    def const(shape, row_block=0):
        return pl.BlockSpec(shape, lambda i: (row_block, 0), pipeline_mode=pl.Buffered(1))

    def slab(w):
        return pl.BlockSpec((None,) + w.shape[1:], lambda i: (layer, 0, 0), pipeline_mode=pl.Buffered(1))

    consts = (g.reshape(1, d), w_gate, w_val, conv_w, conv_b.reshape(1, dff), w_down, final_g.reshape(1, d))
    act_args = [a for act in acts for a in (act, act, act)]
    act_specs = [sp for _ in acts for sp in tiles(wa)]
    return pl.pallas_call(
        functools.partial(_mix_ffn_body, n_act=len(acts), tiles_per_seq=s // tm, final_norm=final_norm),
        grid=(t // tm,),
        in_specs=(act_specs + [const((wa, d), k) for k in range(len(acts))] + tiles(d)
                  + [slab(c) if c.ndim == 3 else const(c.shape) for c in consts]),
        out_specs=pl.BlockSpec((tm, d), lambda i: (i, 0)),
        out_shape=jax.ShapeDtypeStruct((t, d), F32),
        compiler_params=_cparams("parallel"),
        name="mix_ffn",
    )(*act_args, *([w_out] * len(acts)), x2, x2, x2, *consts)


def _mla_prep_body(x_ref, g_ref, wdq_ref, qg_ref, wuq_ref, wuqs_ref, wdkv_ref, kvg_ref, wk_ref, wvt_ref,
                   ct_ref, st_ref, q_out, k_out, vt_out, kn_out, *, heads_per_group):
    hn = _rms(x_ref[...], g_ref[...]).astype(BF16)
    cq = _rms(_dot(hn, wdq_ref[...]), qg_ref[...]).astype(BF16)
    kva = _dot(hn, wdkv_ref[...])
    ckv = _rms(kva[:, :MLA_KV_RANK], kvg_ref[...]).astype(BF16)
    ct = ct_ref[...]
    st = st_ref[...]
    kr = kva[:, MLA_KV_RANK:MLA_KV_RANK + LANES]
    kr_sw = kva[:, MLA_KV_RANK + LANES:]
    roped = kr * ct[:, :LANES] + kr_sw * st[:, :LANES]
    shift_lane = lax.broadcasted_iota(jnp.int32, (1, LANES), 1) == MLA_AUG_LANE
    roped = jnp.where(shift_lane, 1.0, roped)
    roped2 = jnp.concatenate([roped, roped], axis=1)
    w2 = 2 * LANES
    lane = lax.broadcasted_iota(jnp.int32, (1, LANES), 1)
    kn_rows = [jnp.zeros((1, LANES), F32) for _ in range(MLA_HEADS // heads_per_group)]
    for j in range(0, q_out.shape[1], w2):
        qh = _dot(cq, wuq_ref[:, j:j + w2])
        qs = _dot(cq, wuqs_ref[:, j:j + w2])
        q_out[:, j:j + w2] = ((qh * ct + qs * st) * MLA_SOFTMAX_SCALE).astype(q_out.dtype)
        kb = (_dot(ckv, wk_ref[:, j:j + w2]) + roped2).astype(k_out.dtype)
        k_out[:, j:j + w2] = kb
        k2 = jnp.square(kb.astype(F32))
        for half in range(2):
            hh = j // LANES + half
            n = jnp.sum(jnp.where(shift_lane, 0.0, k2[:, half * LANES:(half + 1) * LANES]), axis=-1, keepdims=True)
            n = jnp.max(n, axis=0, keepdims=True)
            grp = hh // heads_per_group
            kn_rows[grp] = jnp.where(lane == hh % heads_per_group, n, kn_rows[grp])
    for grp, row in enumerate(kn_rows):
        kn_out[0, grp] = jnp.broadcast_to(row, kn_out.shape[2:])
    vt = lax.dot_general(wvt_ref[...], ckv, NT_DIMS, preferred_element_type=F32)
    ones_row = lax.broadcasted_iota(jnp.int32, (vt.shape[0], 1), 0) % MLA_VT_ROWS == MLA_V
    vt_out[0] = jnp.where(ones_row, 1.0, vt).astype(vt_out.dtype)


def _mla_prep(x2, s, g, wdq, qg, wuq, wuqs, wdkv, kvg, wk, wvt, ct, st, *, heads_per_group, tm=1024):
    t, d = x2.shape
    nblk = s // tm
    full = lambda a: pl.BlockSpec(a.shape, lambda i: (0, 0), pipeline_mode=pl.Buffered(1))
    hw = MLA_HEADS * LANES
    groups = MLA_HEADS // heads_per_group
    return pl.pallas_call(
        functools.partial(_mla_prep_body, heads_per_group=heads_per_group),
        grid=(t // tm,),
        in_specs=[pl.BlockSpec((tm, d), lambda i: (i, 0)), full(g), full(wdq), full(qg), full(wuq), full(wuqs),
                  full(wdkv), full(kvg), full(wk), full(wvt),
                  pl.BlockSpec((tm, 2 * LANES), lambda i: (i % nblk, 0)),
                  pl.BlockSpec((tm, 2 * LANES), lambda i: (i % nblk, 0))],
        out_specs=[pl.BlockSpec((tm, hw), lambda i: (i, 0)),
                   pl.BlockSpec((tm, hw), lambda i: (i, 0)),
                   pl.BlockSpec((1, MLA_HEADS * MLA_VT_ROWS, tm), lambda i: (i // nblk, 0, i % nblk)),
                   pl.BlockSpec((1, groups, 8, LANES), lambda i: (i, 0, 0, 0))],
        out_shape=[jax.ShapeDtypeStruct((t, hw), BF16),
                   jax.ShapeDtypeStruct((t, hw), BF16),
                   jax.ShapeDtypeStruct((t // s, MLA_HEADS * MLA_VT_ROWS, s), BF16),
                   jax.ShapeDtypeStruct((t // tm, groups, 8, LANES), F32)],
        compiler_params=_cparams("parallel"),
        name="mla_prep",
    )(x2, g, wdq, qg, wuq, wuqs, wdkv, kvg, wk, wvt, ct, st)


def _mla_attn_body(q_ref, k_ref, vt_ref, kn_ref, o_ref, t_ref, *, heads):
    shift_lane = lax.broadcasted_iota(jnp.int32, (1, LANES), 1) == MLA_AUG_LANE
    kmax = jnp.sqrt(jnp.max(kn_ref[:, 0, 0:1, :], axis=0))

    def values(hh, et, halves):
        ot = _dot(vt_ref[0, hh * MLA_VT_ROWS:(hh + 1) * MLA_VT_ROWS, :], et)
        denom = ot[MLA_V:MLA_V + 1]
        halves.append(ot[:MLA_V] / denom)
        if hh % 2 == 1:
            p = hh // 2
            o_ref[0, :, p * LANES:(p + 1) * LANES] = jnp.concatenate(halves, axis=0).T.astype(o_ref.dtype)
            halves.clear()
        return denom

    lmin = None
    halves = []
    for hh in range(heads):
        q = q_ref[0, :, hh * LANES:(hh + 1) * LANES]
        k = k_ref[0, :, hh * LANES:(hh + 1) * LANES]
        qn = jnp.sqrt(jnp.sum(jnp.square(q.astype(F32)), axis=-1, keepdims=True))
        shift = (qn * (kmax[:, hh:hh + 1] * -BOUND_SLACK)).astype(BF16)
        tt = lax.dot_general(k, jnp.where(shift_lane, shift, q), NT_DIMS, preferred_element_type=F32)
        denom = values(hh, jnp.exp2(tt).astype(BF16), halves)
        lmin = denom if lmin is None else jnp.minimum(lmin, denom)

    @pl.when(jnp.logical_not(jnp.min(lmin) >= DENOM_MIN))
    def _():
        halves = []
        for hh in range(heads):
            q = q_ref[0, :, hh * LANES:(hh + 1) * LANES]
            k = k_ref[0, :, hh * LANES:(hh + 1) * LANES]
            t_ref[...] = lax.dot_general(k, q, NT_DIMS, preferred_element_type=F32)
            m = jnp.max(t_ref[...], axis=0, keepdims=True)
            values(hh, jnp.exp2(t_ref[...] - m).astype(BF16), halves)


def _mla_attention(q, k, vt, kn, *, heads, tq=512):
    b, s, _ = q.shape
    groups = MLA_HEADS // heads
    return pl.pallas_call(
        functools.partial(_mla_attn_body, heads=heads),
        grid=(b, groups, s // tq),
        in_specs=[pl.BlockSpec((1, tq, heads * LANES), lambda bi, g, qi: (bi, qi, g)),
                  pl.BlockSpec((1, s, heads * LANES), lambda bi, g, qi: (bi, 0, g)),
                  pl.BlockSpec((1, heads * MLA_VT_ROWS, s), lambda bi, g, qi: (bi, g, 0)),
                  pl.BlockSpec((kn.shape[0] // b, 1, 8, LANES), lambda bi, g, qi: (bi, g, 0, 0))],
        out_specs=pl.BlockSpec((1, tq, heads * MLA_V), lambda bi, g, qi: (bi, qi, g)),
        out_shape=jax.ShapeDtypeStruct((b, s, MLA_HEADS * MLA_V), BF16),
        scratch_shapes=[pltpu.VMEM((s, tq), F32)],
        compiler_params=_cparams("parallel", "parallel", "parallel"),
        name="mla_attn",
    )(q, k, vt, kn)


def _mla_weights(w_uq, w_dkv, w_ukv):
    f = MLA_NOPE + MLA_ROPE
    half = MLA_ROPE // 2
    r = w_uq.shape[0]
    uq = w_uq.reshape(r, MLA_HEADS, f)
    zq = jnp.zeros((r, MLA_HEADS, LANES - f), w_uq.dtype)
    wuq = jnp.concatenate([uq, zq], axis=-1).reshape(r, MLA_HEADS * LANES)
    wuqs = jnp.concatenate([jnp.zeros((r, MLA_HEADS, MLA_NOPE), w_uq.dtype),
                            uq[..., MLA_NOPE + half:], uq[..., MLA_NOPE:MLA_NOPE + half], zq],
                           axis=-1).reshape(r, MLA_HEADS * LANES)
    d = w_dkv.shape[0]
    kr = w_dkv[:, MLA_KV_RANK:]
    z64 = jnp.zeros((d, MLA_NOPE), w_dkv.dtype)
    z32 = jnp.zeros((d, LANES - f), w_dkv.dtype)
    wdkv = jnp.concatenate([w_dkv[:, :MLA_KV_RANK], z64, kr, z32,
                            z64, kr[:, half:], kr[:, :half], z32], axis=-1)
    c = w_ukv.shape[0]
    ukv = w_ukv.reshape(c, MLA_HEADS, MLA_NOPE + MLA_V)
    wk = jnp.concatenate([ukv[..., :MLA_NOPE], jnp.zeros((c, MLA_HEADS, LANES - MLA_NOPE), w_ukv.dtype)],
                         axis=-1).reshape(c, MLA_HEADS * LANES)
    wvt = jnp.concatenate([ukv[..., MLA_NOPE:].transpose(1, 2, 0),
                           jnp.zeros((MLA_HEADS, MLA_VT_ROWS - MLA_V, c), w_ukv.dtype)], axis=1)
    wvt = wvt.reshape(MLA_HEADS * MLA_VT_ROWS, c)
    return wuq.astype(BF16), wuqs.astype(BF16), wdkv.astype(BF16), wk.astype(BF16), wvt.astype(BF16)


def _rope_lane_tables(s):
    inv = ROPE_THETA ** (-jnp.arange(0, MLA_ROPE, 2, dtype=F32) / MLA_ROPE)
    ang = jnp.arange(s, dtype=F32)[:, None] * inv[None, :]
    cos, sin = jnp.cos(ang), jnp.sin(ang)
    one = jnp.ones((s, MLA_NOPE), F32)
    z64 = jnp.zeros((s, MLA_NOPE), F32)
    z32 = jnp.zeros((s, LANES - MLA_NOPE - MLA_ROPE), F32)
    ct = jnp.concatenate([one, cos, cos, z32], axis=-1)
    st = jnp.concatenate([z64, -sin, sin, z32], axis=-1)
    return jnp.tile(ct, (1, 2)), jnp.tile(st, (1, 2))


def kernel(x, mix_norm_e, w_in_e, diff_lq1, diff_lk1, diff_lq2, diff_lk2, diff_subln_g, na_rpb, w_out_e,
           mix_norm_o, w_dq, q_norm_g, w_uq, w_dkv, kv_norm_g, w_ukv, w_o_mla, ffn_norm_g, w_ffn_gate,
           w_ffn_val, ffn_conv_w, ffn_conv_b, w_ffn_down, final_norm_g):
    b, s, d = x.shape
    t = b * s
    depth = ffn_norm_g.shape[0]
    wg_all, wv_all, wd_all = w_ffn_gate.astype(BF16), w_ffn_val.astype(BF16), w_ffn_down.astype(BF16)
    for i in range(depth):
        j = i // 2
        x2 = x.reshape(t, d)
        if i % 2 == 0:
            lam_init = 0.8 - 0.6 * math.exp(-0.3 * i)
            wq = DIFF_HEADS * LANES
            col_scale = np.ones((1, w_in_e.shape[-1]), np.float32)
            col_scale[:, :wq] = SOFTMAX_SCALE
            col_scale[:, 3 * wq:3 * wq + NA_HEADS * HEAD_DIM] = SOFTMAX_SCALE
            proj, kn_d = _norm_proj(x2, mix_norm_e[j], w_in_e[j].astype(BF16), jnp.asarray(col_scale))
            proj = proj.reshape(b, s, -1)
            lvec = jnp.stack([diff_lq1[j], diff_lk1[j], diff_lq2[j], diff_lk2[j]]).astype(F32)
            a = _diff_attention(proj, kn_d, lvec, diff_subln_g[j], lam_init=lam_init)
            nb = _na_attention(proj, na_rpb[j], col0=3 * DIFF_HEADS)
            acts, w_out = [a.reshape(t, -1), nb.reshape(t, -1)], w_out_e[j].astype(BF16)
        else:
            wuq, wuqs, wdkv, wk, wvt = _mla_weights(w_uq[j], w_dkv[j], w_ukv[j])
            ct, st = _rope_lane_tables(s)
            q, k, vt, kn = _mla_prep(x2, s, mix_norm_o[j].reshape(1, d), w_dq[j].astype(BF16),
                                     q_norm_g[j].reshape(1, -1), wuq, wuqs, wdkv, kv_norm_g[j].reshape(1, -1),
                                     wk, wvt, ct, st, heads_per_group=MLA_HEADS_PER_STEP)
            o = _mla_attention(q.reshape(b, s, -1), k.reshape(b, s, -1), vt, kn, heads=MLA_HEADS_PER_STEP)
            acts, w_out = [o.reshape(t, -1)], w_o_mla[j].astype(BF16)
        x = _mix_ffn(acts, w_out, x2, s, ffn_norm_g[i], wg_all, wv_all, ffn_conv_w[i], ffn_conv_b[i], wd_all,
                     final_norm_g, layer=i, final_norm=(i == depth - 1)).reshape(b, s, d)
    return x
```

```python
import functools
import math

import numpy as np
import jax
import jax.numpy as jnp
from jax import lax
from jax.experimental import pallas as pl
from jax.experimental.pallas import tpu as pltpu

F32 = jnp.float32
BF16 = jnp.bfloat16

EPS = 1e-6
NEG_INF = -1e30
GRID_W = 64
HEAD_DIM = 64
DIFF_HEADS = 4
NA_HEADS = 8
NA_WIN_H = 8
NA_WIN_W = 16
MLA_HEADS = 16
MLA_NOPE = 64
MLA_ROPE = 32
MLA_V = 64
MLA_KV_RANK = 256
ROPE_THETA = 10000.0

LANES = 128
VMEM_LIMIT = 56 * 1024 * 1024
NT_DIMS = (((1,), (1,)), ((), ()))
LOG2E = math.log2(math.e)
SQRT_HALF = np.float32(math.sqrt(0.5))
SOFTMAX_SCALE = HEAD_DIM ** -0.5 * LOG2E
MLA_SOFTMAX_SCALE = (MLA_NOPE + MLA_ROPE) ** -0.5 * LOG2E
BOUND_SLACK = 1.0 + 2.0 ** -6
DENOM_MIN = 2.0 ** -60
MLA_AUG_LANE = MLA_NOPE + MLA_ROPE
MLA_HEADS_PER_STEP = 8
MLA_VT_ROWS = 80


def _cparams(*sem):
    return pltpu.CompilerParams(dimension_semantics=sem, vmem_limit_bytes=VMEM_LIMIT)


def _rms(xf, g):
    ms = jnp.mean(xf * xf, axis=-1, keepdims=True)
    return xf * lax.rsqrt(ms + EPS) * g


def _dot(a, b):
    return jnp.dot(a, b, preferred_element_type=F32)


def _low_lanes():
    return lax.broadcasted_iota(jnp.int32, (1, LANES), 1) < HEAD_DIM


def _exp_weights(t):
    return jnp.exp2(t - jnp.max(t, axis=-1, keepdims=True)).astype(BF16)


def _fill_ones_aug(vaug_ref, v_ref, n_blocks):
    ones = jnp.ones((v_ref.shape[1], LANES), BF16)
    for p in range(n_blocks):
        vaug_ref[:, 2 * p * LANES:(2 * p + 1) * LANES] = v_ref[0, :, p * LANES:(p + 1) * LANES]
        vaug_ref[:, (2 * p + 1) * LANES:(2 * p + 2) * LANES] = ones


def _norm_proj_body(x_ref, g_ref, w_ref, cs_ref, o_ref, kn_out, *, n_chunk):
    hn = _rms(x_ref[...], g_ref[...]).astype(BF16)
    lo = _low_lanes()
    lane = lax.broadcasted_iota(jnp.int32, (1, LANES), 1)
    for j in range(0, o_ref.shape[1], n_chunk):
        ob = (_dot(hn, w_ref[:, j:j + n_chunk]) * cs_ref[:, j:j + n_chunk]).astype(o_ref.dtype)
        o_ref[:, j:j + n_chunk] = ob
        if j == n_chunk:
            k2 = jnp.square(ob.astype(F32))
            kn = jnp.zeros((1, LANES), F32)
            for h in range(DIFF_HEADS):
                kh = k2[:, h * LANES:(h + 1) * LANES]
                n1 = jnp.max(jnp.sum(jnp.where(lo, kh, 0.0), axis=-1, keepdims=True), axis=0, keepdims=True)
                n2 = jnp.max(jnp.sum(jnp.where(lo, 0.0, kh), axis=-1, keepdims=True), axis=0, keepdims=True)
                kn = jnp.where(lane == 2 * h, n1, jnp.where(lane == 2 * h + 1, n2, kn))
            kn_out[0] = jnp.broadcast_to(kn, kn_out.shape[1:])


def _norm_proj(x2, g, w, col_scale, *, tm=1024):
    t, d = x2.shape
    n = w.shape[1]
    return pl.pallas_call(
        functools.partial(_norm_proj_body, n_chunk=DIFF_HEADS * LANES),
        grid=(t // tm,),
        in_specs=[pl.BlockSpec((tm, d), lambda i: (i, 0)),
                  pl.BlockSpec((1, d), lambda i: (0, 0)),
                  pl.BlockSpec((d, n), lambda i: (0, 0)),
                  pl.BlockSpec((1, n), lambda i: (0, 0))],
        out_specs=[pl.BlockSpec((tm, n), lambda i: (i, 0)),
                   pl.BlockSpec((1, 8, LANES), lambda i: (i, 0, 0))],
        out_shape=[jax.ShapeDtypeStruct((t, n), BF16),
                   jax.ShapeDtypeStruct((t // tm, 8, LANES), F32)],
        compiler_params=_cparams("parallel"),
        name="norm_proj",
    )(x2, g.reshape(1, d), w, col_scale)


def _diff_attn_body(lvec_ref, q_ref, k_ref, v_ref, kn_ref, g_ref, o_ref, vaug_ref, t_ref, *, tq, lam_init):
    qi = pl.program_id(1)
    lo = _low_lanes()

    @pl.when(qi == 0)
    def _():
        _fill_ones_aug(vaug_ref, v_ref, DIFF_HEADS)

    kmax = jnp.sqrt(jnp.max(kn_ref[:, 0:1, :], axis=0)) * BOUND_SLACK

    s_len = k_ref.shape[1]
    row = qi * tq + lax.broadcasted_iota(jnp.int32, (tq, s_len), 0)
    col = lax.broadcasted_iota(jnp.int32, (tq, s_len), 1)
    dist = jnp.abs(row - col).astype(F32)
    lv = lvec_ref[...]
    lam = (jnp.exp(jnp.sum(lv[0:1] * lv[1:2], axis=-1, keepdims=True))
           - jnp.exp(jnp.sum(lv[2:3] * lv[3:4], axis=-1, keepdims=True)) + lam_init)

    def maps(h):
        q = q_ref[0, :, h * LANES:(h + 1) * LANES]
        k = k_ref[0, :, h * LANES:(h + 1) * LANES]
        zero = jnp.zeros_like(q)
        bias = (-LOG2E * 2.0 ** (-8.0 * (h + 1) / DIFF_HEADS)) * dist
        s1 = lax.dot_general(jnp.where(lo, q, zero), k, NT_DIMS, preferred_element_type=F32)
        s2 = lax.dot_general(jnp.where(lo, zero, q), k, NT_DIMS, preferred_element_type=F32)
        return q, s1, s2, bias

    def finish(h, o1, o2):
        o = o1[:, :LANES] / o1[:, LANES:] - lam * (o2[:, :LANES] / o2[:, LANES:])
        o_ref[0, :, h * LANES:(h + 1) * LANES] = (_rms(o, g_ref[...]) * (1.0 - lam_init)).astype(o_ref.dtype)

    lmin = None
    for h in range(DIFF_HEADS):
        q, s1, s2, bias = maps(h)
        q2 = jnp.square(q.astype(F32))
        b1 = (jnp.sqrt(jnp.sum(jnp.where(lo, q2, 0.0), axis=-1, keepdims=True))
              * kmax[:, 2 * h:2 * h + 1])
        b2 = (jnp.sqrt(jnp.sum(jnp.where(lo, 0.0, q2), axis=-1, keepdims=True))
              * kmax[:, 2 * h + 1:2 * h + 2])
        vaug = vaug_ref[:, 2 * h * LANES:(2 * h + 2) * LANES]
        o1 = _dot(jnp.exp2(s1 + (bias - b1)).astype(BF16), vaug)
        o2 = _dot(jnp.exp2(s2 + (bias - b2)).astype(BF16), vaug)
        finish(h, o1, o2)
        l = jnp.minimum(o1[:, LANES:], o2[:, LANES:])
        lmin = l if lmin is None else jnp.minimum(lmin, l)

    @pl.when(jnp.logical_not(jnp.min(lmin) >= DENOM_MIN))
    def _():
        for h in range(DIFF_HEADS):
            _, s1, s2, bias = maps(h)
            t_ref[0] = s1 + bias
            t_ref[1] = s2 + bias
            vaug = vaug_ref[:, 2 * h * LANES:(2 * h + 2) * LANES]
            finish(h, _dot(_exp_weights(t_ref[0]), vaug), _dot(_exp_weights(t_ref[1]), vaug))


def _diff_attention(proj, kn, lvec, subln_g, *, lam_init, tq=512):
    b, s, _ = proj.shape
    w = DIFF_HEADS * LANES
    return pl.pallas_call(
        functools.partial(_diff_attn_body, tq=tq, lam_init=lam_init),
        grid=(b, s // tq),
        in_specs=[pl.BlockSpec((4, HEAD_DIM), lambda bi, qi: (0, 0)),
                  pl.BlockSpec((1, tq, w), lambda bi, qi: (bi, qi, 0)),
                  pl.BlockSpec((1, s, w), lambda bi, qi: (bi, 0, 1)),
                  pl.BlockSpec((1, s, w), lambda bi, qi: (bi, 0, 2)),
                  pl.BlockSpec((kn.shape[0] // b, 8, LANES), lambda bi, qi: (bi, 0, 0)),
                  pl.BlockSpec((1, LANES), lambda bi, qi: (0, 0))],
        out_specs=pl.BlockSpec((1, tq, w), lambda bi, qi: (bi, qi, 0)),
        out_shape=jax.ShapeDtypeStruct((b, s, w), BF16),
        scratch_shapes=[pltpu.VMEM((s, 2 * w), BF16), pltpu.VMEM((2, tq, s), F32)],
        compiler_params=_cparams("parallel", "arbitrary"),
        name="diff_attn",
    )(lvec, proj, proj, proj, kn, subln_g.reshape(1, LANES))


def _na_bias_table(rpb, kh):
    qc = np.arange(GRID_W)[:, None]
    kc = np.arange(GRID_W)[None, :]
    wstart = np.clip(qc - NA_WIN_W // 2, 0, GRID_W - NA_WIN_W)
    valid = (kc >= wstart) & (kc < wstart + NA_WIN_W)
    dc = kc - qc + NA_WIN_W - 1
    onehot = ((dc[None] == np.arange(2 * NA_WIN_W - 1)[:, None, None]) & valid[None]).astype(np.float32)
    toep = jnp.einsum("hrc,cqk->hrqk", rpb.astype(F32) * LOG2E, onehot, precision=lax.Precision.HIGHEST)
    toep = jnp.where(valid[None, None], toep, NEG_INF)
    per_d = [toep[:, NA_WIN_H - 1 - d:NA_WIN_H - 1 - d + kh].transpose(0, 2, 1, 3) for d in range(kh)]
    return jnp.stack(per_d, axis=1).reshape(rpb.shape[0], kh, GRID_W, kh * GRID_W)


def _na_body(q_ref, k_ref, v_ref, bias_ref, o_ref, vaug_ref, *, rows_n, kh):
    lo = _low_lanes()
    _fill_ones_aug(vaug_ref, v_ref, 1)
    for r in range(rows_n):
        rs = min(max(r - kh // 2, 0), rows_n - kh)
        q = q_ref[0, r * GRID_W:(r + 1) * GRID_W, :]
        kwin = k_ref[0, rs * GRID_W:(rs + kh) * GRID_W, :]
        vwin = vaug_ref[rs * GRID_W:(rs + kh) * GRID_W, :]
        zero = jnp.zeros_like(q)
        q2 = jnp.concatenate([jnp.where(lo, q, zero), jnp.where(lo, zero, q)], axis=0)
        bias = bias_ref[:, r - rs].reshape(2 * GRID_W, kh * GRID_W)
        t = lax.dot_general(q2, kwin, NT_DIMS, preferred_element_type=F32) + bias
        o2 = _dot(_exp_weights(t), vwin)
        o2 = o2[:, :LANES] / o2[:, LANES:]
        o_ref[0, r * GRID_W:(r + 1) * GRID_W, :] = jnp.where(lo, o2[:GRID_W], o2[GRID_W:]).astype(o_ref.dtype)


def _na_attention(proj, rpb, *, col0):
    b, s, _ = proj.shape
    rows_n = s // GRID_W
    kh = min(NA_WIN_H, rows_n)
    pairs = NA_HEADS // 2
    bias = _na_bias_table(rpb, kh)
    return pl.pallas_call(
        functools.partial(_na_body, rows_n=rows_n, kh=kh),
        grid=(pairs, b),
        in_specs=[pl.BlockSpec((1, s, LANES), lambda p, bi: (bi, 0, col0 + p)),
                  pl.BlockSpec((1, s, LANES), lambda p, bi: (bi, 0, col0 + pairs + p)),
                  pl.BlockSpec((1, s, LANES), lambda p, bi: (bi, 0, col0 + 2 * pairs + p)),
                  pl.BlockSpec((2, kh, GRID_W, kh * GRID_W), lambda p, bi: (p, 0, 0, 0))],
        out_specs=pl.BlockSpec((1, s, LANES), lambda p, bi: (bi, 0, p)),
        out_shape=jax.ShapeDtypeStruct((b, s, pairs * LANES), BF16),
        scratch_shapes=[pltpu.VMEM((s, 2 * LANES), BF16)],
        compiler_params=_cparams("parallel", "parallel"),
        name="na_attn",
    )(proj, proj, proj, bias)


HALO = 16


def _mix_ffn_body(*refs, n_act, tiles_per_seq, final_norm):
    acts = refs[:3 * n_act]
    wos = refs[3 * n_act:4 * n_act]
    x_ref, xp_ref, xn_ref, g_ref, wg_ref, wv_ref, cw_ref, cb_ref, wd_ref, fg_ref, o_ref = refs[4 * n_act:]
    pos = pl.program_id(0) % tiles_per_seq
    tm = x_ref.shape[0]
    xe = jnp.concatenate([xp_ref[...], x_ref[...], xn_ref[...]], axis=0)
    for k in range(n_act):
        a_ref, ap_ref, an_ref = acts[3 * k:3 * k + 3]
        xe = xe + _dot(jnp.concatenate([ap_ref[...], a_ref[...], an_ref[...]], axis=0), wos[k][...])
    row = lax.broadcasted_iota(jnp.int32, (tm + 2 * HALO, 1), 0)
    first_kept = jnp.where(pos == 0, HALO, 0)
    end_kept = jnp.where(pos == tiles_per_seq - 1, HALO + tm, tm + 2 * HALO)
    xe = jnp.where((row >= first_kept) & (row < end_kept), xe, 0.0)
    y = _rms(xe, g_ref[...])
    a_ext = _dot(y.astype(BF16), wg_ref[...])
    cw = cw_ref[...]
    a = (cw[0:1] * a_ext[HALO - 1:HALO - 1 + tm] + cw[1:2] * a_ext[HALO:HALO + tm]
         + cw[2:3] * a_ext[HALO + 1:HALO + 1 + tm] + cb_ref[...])
    gate = 0.5 * a * (1.0 + lax.erf(a * SQRT_HALF))
    val = _dot(y[HALO:HALO + tm].astype(BF16), wv_ref[...])
    out = xe[HALO:HALO + tm] + _dot((gate * val).astype(BF16), wd_ref[...])
    o_ref[...] = _rms(out, fg_ref[...]) if final_norm else out


def _mix_ffn(acts, w_out, x2, s, g, w_gate, w_val, conv_w, conv_b, w_down, final_g, *, layer, final_norm, tm=512):
    t, d = x2.shape
    dff = w_gate.shape[2]
    wa = acts[0].shape[1]
    assert all(a.shape[1] == wa for a in acts) and wa * len(acts) == w_out.shape[0]
    hb = tm // HALO
    last = t // HALO - 1

    def tiles(width):
        return [pl.BlockSpec((tm, width), lambda i: (i, 0)),
                pl.BlockSpec((HALO, width), lambda i: (jnp.maximum(i * hb - 1, 0), 0)),
                pl.BlockSpec((HALO, width), lambda i: (jnp.minimum((i + 1) * hb, last), 0))]

    def const(shape, row_block=0):
        return pl.BlockSpec(shape, lambda i: (row_block, 0), pipeline_mode=pl.Buffered(1))

    def slab(w):
        return pl.BlockSpec((None,) + w.shape[1:], lambda i: (layer, 0, 0), pipeline_mode=pl.Buffered(1))

    consts = (g.reshape(1, d), w_gate, w_val, conv_w, conv_b.reshape(1, dff), w_down, final_g.reshape(1, d))
    act_args = [a for act in acts for a in (act, act, act)]
    act_specs = [sp for _ in acts for sp in tiles(wa)]
    return pl.pallas_call(
        functools.partial(_mix_ffn_body, n_act=len(acts), tiles_per_seq=s // tm, final_norm=final_norm),
        grid=(t // tm,),
        in_specs=(act_specs + [const((wa, d), k) for k in range(len(acts))] + tiles(d)
                  + [slab(c) if c.ndim == 3 else const(c.shape) for c in consts]),
        out_specs=pl.BlockSpec((tm, d), lambda i: (i, 0)),
        out_shape=jax.ShapeDtypeStruct((t, d), F32),
        compiler_params=_cparams("parallel"),
        name="mix_ffn",
    )(*act_args, *([w_out] * len(acts)), x2, x2, x2, *consts)


def _mla_prep_body(x_ref, g_ref, wdq_ref, qg_ref, wuq_ref, wuqs_ref, wdkv_ref, kvg_ref, wk_ref, wvt_ref,
                   ct_ref, st_ref, q_out, k_out, vt_out, kn_out, *, heads_per_group):
    hn = _rms(x_ref[...], g_ref[...]).astype(BF16)
    cq = _rms(_dot(hn, wdq_ref[...]), qg_ref[...]).astype(BF16)
    kva = _dot(hn, wdkv_ref[...])
    ckv = _rms(kva[:, :MLA_KV_RANK], kvg_ref[...]).astype(BF16)
    ct = ct_ref[...]
    st = st_ref[...]
    kr = kva[:, MLA_KV_RANK:MLA_KV_RANK + LANES]
    kr_sw = kva[:, MLA_KV_RANK + LANES:]
    roped = kr * ct[:, :LANES] + kr_sw * st[:, :LANES]
    shift_lane = lax.broadcasted_iota(jnp.int32, (1, LANES), 1) == MLA_AUG_LANE
    roped = jnp.where(shift_lane, 1.0, roped)
    roped2 = jnp.concatenate([roped, roped], axis=1)
    w2 = 2 * LANES
    lane = lax.broadcasted_iota(jnp.int32, (1, LANES), 1)
    kn_rows = [jnp.zeros((1, LANES), F32) for _ in range(MLA_HEADS // heads_per_group)]
    for j in range(0, q_out.shape[1], w2):
        qh = _dot(cq, wuq_ref[:, j:j + w2])
        qs = _dot(cq, wuqs_ref[:, j:j + w2])
        q_out[:, j:j + w2] = ((qh * ct + qs * st) * MLA_SOFTMAX_SCALE).astype(q_out.dtype)
        kb = (_dot(ckv, wk_ref[:, j:j + w2]) + roped2).astype(k_out.dtype)
        k_out[:, j:j + w2] = kb
        k2 = jnp.square(kb.astype(F32))
        for half in range(2):
            hh = j // LANES + half
            n = jnp.sum(jnp.where(shift_lane, 0.0, k2[:, half * LANES:(half + 1) * LANES]), axis=-1, keepdims=True)
            n = jnp.max(n, axis=0, keepdims=True)
            grp = hh // heads_per_group
            kn_rows[grp] = jnp.where(lane == hh % heads_per_group, n, kn_rows[grp])
    for grp, row in enumerate(kn_rows):
        kn_out[0, grp] = jnp.broadcast_to(row, kn_out.shape[2:])
    vt = lax.dot_general(wvt_ref[...], ckv, NT_DIMS, preferred_element_type=F32)
    ones_row = lax.broadcasted_iota(jnp.int32, (vt.shape[0], 1), 0) % MLA_VT_ROWS == MLA_V
    vt_out[0] = jnp.where(ones_row, 1.0, vt).astype(vt_out.dtype)


def _mla_prep(x2, s, g, wdq, qg, wuq, wuqs, wdkv, kvg, wk, wvt, ct, st, *, heads_per_group, tm=1024):
    t, d = x2.shape
    nblk = s // tm
    full = lambda a: pl.BlockSpec(a.shape, lambda i: (0, 0), pipeline_mode=pl.Buffered(1))
    hw = MLA_HEADS * LANES
    groups = MLA_HEADS // heads_per_group
    return pl.pallas_call(
        functools.partial(_mla_prep_body, heads_per_group=heads_per_group),
        grid=(t // tm,),
        in_specs=[pl.BlockSpec((tm, d), lambda i: (i, 0)), full(g), full(wdq), full(qg), full(wuq), full(wuqs),
                  full(wdkv), full(kvg), full(wk), full(wvt),
                  pl.BlockSpec((tm, 2 * LANES), lambda i: (i % nblk, 0)),
                  pl.BlockSpec((tm, 2 * LANES), lambda i: (i % nblk, 0))],
        out_specs=[pl.BlockSpec((tm, hw), lambda i: (i, 0)),
                   pl.BlockSpec((tm, hw), lambda i: (i, 0)),
                   pl.BlockSpec((1, MLA_HEADS * MLA_VT_ROWS, tm), lambda i: (i // nblk, 0, i % nblk)),
                   pl.BlockSpec((1, groups, 8, LANES), lambda i: (i, 0, 0, 0))],
        out_shape=[jax.ShapeDtypeStruct((t, hw), BF16),
                   jax.ShapeDtypeStruct((t, hw), BF16),
                   jax.ShapeDtypeStruct((t // s, MLA_HEADS * MLA_VT_ROWS, s), BF16),
                   jax.ShapeDtypeStruct((t // tm, groups, 8, LANES), F32)],
        compiler_params=_cparams("parallel"),
        name="mla_prep",
    )(x2, g, wdq, qg, wuq, wuqs, wdkv, kvg, wk, wvt, ct, st)


def _mla_attn_body(q_ref, k_ref, vt_ref, kn_ref, o_ref, t_ref, *, heads):
    shift_lane = lax.broadcasted_iota(jnp.int32, (1, LANES), 1) == MLA_AUG_LANE
    kmax = jnp.sqrt(jnp.max(kn_ref[:, 0, 0:1, :], axis=0))

    def values(hh, et, halves):
        ot = _dot(vt_ref[0, hh * MLA_VT_ROWS:(hh + 1) * MLA_VT_ROWS, :], et)
        denom = ot[MLA_V:MLA_V + 1]
        halves.append(ot[:MLA_V] / denom)
        if hh % 2 == 1:
            p = hh // 2
            o_ref[0, :, p * LANES:(p + 1) * LANES] = jnp.concatenate(halves, axis=0).T.astype(o_ref.dtype)
            halves.clear()
        return denom

    lmin = None
    halves = []
    for hh in range(heads):
        q = q_ref[0, :, hh * LANES:(hh + 1) * LANES]
        k = k_ref[0, :, hh * LANES:(hh + 1) * LANES]
        qn = jnp.sqrt(jnp.sum(jnp.square(q.astype(F32)), axis=-1, keepdims=True))
        shift = (qn * (kmax[:, hh:hh + 1] * -BOUND_SLACK)).astype(BF16)
        tt = lax.dot_general(k, jnp.where(shift_lane, shift, q), NT_DIMS, preferred_element_type=F32)
        denom = values(hh, jnp.exp2(tt).astype(BF16), halves)
        lmin = denom if lmin is None else jnp.minimum(lmin, denom)

    @pl.when(jnp.logical_not(jnp.min(lmin) >= DENOM_MIN))
    def _():
        halves = []
        for hh in range(heads):
            q = q_ref[0, :, hh * LANES:(hh + 1) * LANES]
            k = k_ref[0, :, hh * LANES:(hh + 1) * LANES]
            t_ref[...] = lax.dot_general(k, q, NT_DIMS, preferred_element_type=F32)
            m = jnp.max(t_ref[...], axis=0, keepdims=True)
            values(hh, jnp.exp2(t_ref[...] - m).astype(BF16), halves)


def _mla_attention(q, k, vt, kn, *, heads, tq=512):
    b, s, _ = q.shape
    groups = MLA_HEADS // heads
    return pl.pallas_call(
        functools.partial(_mla_attn_body, heads=heads),
        grid=(b, groups, s // tq),
        in_specs=[pl.BlockSpec((1, tq, heads * LANES), lambda bi, g, qi: (bi, qi, g)),
                  pl.BlockSpec((1, s, heads * LANES), lambda bi, g, qi: (bi, 0, g)),
                  pl.BlockSpec((1, heads * MLA_VT_ROWS, s), lambda bi, g, qi: (bi, g, 0)),
                  pl.BlockSpec((kn.shape[0] // b, 1, 8, LANES), lambda bi, g, qi: (bi, g, 0, 0))],
        out_specs=pl.BlockSpec((1, tq, heads * MLA_V), lambda bi, g, qi: (bi, qi, g)),
        out_shape=jax.ShapeDtypeStruct((b, s, MLA_HEADS * MLA_V), BF16),
        scratch_shapes=[pltpu.VMEM((s, tq), F32)],
        compiler_params=_cparams("parallel", "parallel", "parallel"),
        name="mla_attn",
    )(q, k, vt, kn)


def _mla_weights(w_uq, w_dkv, w_ukv):
    f = MLA_NOPE + MLA_ROPE
    half = MLA_ROPE // 2
    r = w_uq.shape[0]
    uq = w_uq.reshape(r, MLA_HEADS, f)
    zq = jnp.zeros((r, MLA_HEADS, LANES - f), w_uq.dtype)
    wuq = jnp.concatenate([uq, zq], axis=-1).reshape(r, MLA_HEADS * LANES)
    wuqs = jnp.concatenate([jnp.zeros((r, MLA_HEADS, MLA_NOPE), w_uq.dtype),
                            uq[..., MLA_NOPE + half:], uq[..., MLA_NOPE:MLA_NOPE + half], zq],
                           axis=-1).reshape(r, MLA_HEADS * LANES)
    d = w_dkv.shape[0]
    kr = w_dkv[:, MLA_KV_RANK:]
    z64 = jnp.zeros((d, MLA_NOPE), w_dkv.dtype)
    z32 = jnp.zeros((d, LANES - f), w_dkv.dtype)
    wdkv = jnp.concatenate([w_dkv[:, :MLA_KV_RANK], z64, kr, z32,
                            z64, kr[:, half:], kr[:, :half], z32], axis=-1)
    c = w_ukv.shape[0]
    ukv = w_ukv.reshape(c, MLA_HEADS, MLA_NOPE + MLA_V)
    wk = jnp.concatenate([ukv[..., :MLA_NOPE], jnp.zeros((c, MLA_HEADS, LANES - MLA_NOPE), w_ukv.dtype)],
                         axis=-1).reshape(c, MLA_HEADS * LANES)
    wvt = jnp.concatenate([ukv[..., MLA_NOPE:].transpose(1, 2, 0),
                           jnp.zeros((MLA_HEADS, MLA_VT_ROWS - MLA_V, c), w_ukv.dtype)], axis=1)
    wvt = wvt.reshape(MLA_HEADS * MLA_VT_ROWS, c)
    return wuq.astype(BF16), wuqs.astype(BF16), wdkv.astype(BF16), wk.astype(BF16), wvt.astype(BF16)


def _rope_lane_tables(s):
    inv = ROPE_THETA ** (-jnp.arange(0, MLA_ROPE, 2, dtype=F32) / MLA_ROPE)
    ang = jnp.arange(s, dtype=F32)[:, None] * inv[None, :]
    cos, sin = jnp.cos(ang), jnp.sin(ang)
    one = jnp.ones((s, MLA_NOPE), F32)
    z64 = jnp.zeros((s, MLA_NOPE), F32)
    z32 = jnp.zeros((s, LANES - MLA_NOPE - MLA_ROPE), F32)
    ct = jnp.concatenate([one, cos, cos, z32], axis=-1)
    st = jnp.concatenate([z64, -sin, sin, z32], axis=-1)
    return jnp.tile(ct, (1, 2)), jnp.tile(st, (1, 2))


def kernel(x, mix_norm_e, w_in_e, diff_lq1, diff_lk1, diff_lq2, diff_lk2, diff_subln_g, na_rpb, w_out_e,
           mix_norm_o, w_dq, q_norm_g, w_uq, w_dkv, kv_norm_g, w_ukv, w_o_mla, ffn_norm_g, w_ffn_gate,
           w_ffn_val, ffn_conv_w, ffn_conv_b, w_ffn_down, final_norm_g):
    b, s, d = x.shape
    t = b * s
    depth = ffn_norm_g.shape[0]
    wg_all, wv_all, wd_all = w_ffn_gate.astype(BF16), w_ffn_val.astype(BF16), w_ffn_down.astype(BF16)
    for i in range(depth):
        j = i // 2
        x2 = x.reshape(t, d)
        if i % 2 == 0:
            lam_init = 0.8 - 0.6 * math.exp(-0.3 * i)
            wq = DIFF_HEADS * LANES
            col_scale = np.ones((1, w_in_e.shape[-1]), np.float32)
            col_scale[:, :wq] = SOFTMAX_SCALE
            col_scale[:, 3 * wq:3 * wq + NA_HEADS * HEAD_DIM] = SOFTMAX_SCALE
            proj, kn_d = _norm_proj(x2, mix_norm_e[j], w_in_e[j].astype(BF16), jnp.asarray(col_scale))
            proj = proj.reshape(b, s, -1)
            lvec = jnp.stack([diff_lq1[j], diff_lk1[j], diff_lq2[j], diff_lk2[j]]).astype(F32)
            a = _diff_attention(proj, kn_d, lvec, diff_subln_g[j], lam_init=lam_init)
            nb = _na_attention(proj, na_rpb[j], col0=3 * DIFF_HEADS)
            acts, w_out = [a.reshape(t, -1), nb.reshape(t, -1)], w_out_e[j].astype(BF16)
        else:
            wuq, wuqs, wdkv, wk, wvt = _mla_weights(w_uq[j], w_dkv[j], w_ukv[j])
            ct, st = _rope_lane_tables(s)
            q, k, vt, kn = _mla_prep(x2, s, mix_norm_o[j].reshape(1, d), w_dq[j].astype(BF16),
                                     q_norm_g[j].reshape(1, -1), wuq, wuqs, wdkv, kv_norm_g[j].reshape(1, -1),
                                     wk, wvt, ct, st, heads_per_group=MLA_HEADS_PER_STEP)
            o = _mla_attention(q.reshape(b, s, -1), k.reshape(b, s, -1), vt, kn, heads=MLA_HEADS_PER_STEP)
            acts, w_out = [o.reshape(t, -1)], w_o_mla[j].astype(BF16)
        x = _mix_ffn(acts, w_out, x2, s, ffn_norm_g[i], wg_all, wv_all, ffn_conv_w[i], ffn_conv_b[i], wd_all,
                     final_norm_g, layer=i, final_norm=(i == depth - 1)).reshape(b, s, d)
    return x
```
